```python
import math
import jax, jax.numpy as jnp
from jax import lax
import numpy as np

D_MODEL = 1024
BATCH = 32
SEQ = 2048
DEPTH = 2
DEC_BATCH = 16
DEC_SEQ = 64
PAST_LEN = 1024

CHUNK = 64
MIX_WIDTH = D_MODEL
CONV_CH = MIX_WIDTH // 2
SSM_WIDTH = MIX_WIDTH - CONV_CH
CONV_K = 3
SSM_GROUP = 16
SSM_GROUPS = SSM_WIDTH // SSM_GROUP
SSM_STATE = 64
IN_PROJ_WIDTH = 3 * CONV_CH + SSM_WIDTH
N_MEM = 256
MEM_HEADS = 4
MEM_HEAD_DIM = D_MODEL // MEM_HEADS
D_FF = 4 * D_MODEL
EPS = 1e-6
DT_MIN = 1e-3
DT_MAX = 1e-1
A_RE_MAX = -1e-4

kernel_name = "hybrid_conv_s5_stream_step"


def _rmsnorm(x, g):
    x32 = x.astype(jnp.float32)
    inv = lax.rsqrt(jnp.mean(x32 * x32, axis=-1, keepdims=True) + EPS)
    return (x32 * inv * g.astype(jnp.float32)).astype(x.dtype)


def _short_conv(b_gate, c_gate, v, conv_state, conv_w):
    xin = c_gate * v
    s = xin.shape[1]
    xp = jnp.concatenate([conv_state.astype(xin.dtype), xin], axis=1)
    out = conv_w[0] * xp[:, 0:s]
    for k in range(1, CONV_K):
        out = out + conv_w[k] * xp[:, k:k + s]
    return b_gate * out, xp[:, -(CONV_K - 1):]


def _lin_combine(e1, e2):
    a1, b1 = e1
    a2, b2 = e2
    return (a1 * a2, a2 * b1 + b2)


def _s5(u, h0_re, h0_im, a_re, a_im, log_dt, b_re, b_im, c_re, c_im, d_skip, w_glu):
    f32 = jnp.float32
    bsz, s, _ = u.shape
    u32 = u.astype(f32).reshape(bsz, s, SSM_GROUPS, SSM_GROUP)
    lam = lax.complex(jnp.minimum(a_re.astype(f32), A_RE_MAX), a_im.astype(f32))
    dt = jnp.exp(log_dt.astype(f32))[:, None]
    a_bar = jnp.exp(lam * dt)
    b_mat = lax.complex(b_re.astype(f32), b_im.astype(f32))
    b_bar = ((a_bar - 1.0) / lam)[:, :, None] * b_mat
    bu = jnp.einsum('bsgh,gph->bsgp', u32.astype(jnp.complex64), b_bar)
    h0 = lax.complex(h0_re.astype(f32), h0_im.astype(f32))
    bu = bu.at[:, 0].add(a_bar * h0)
    a_el = jnp.broadcast_to(a_bar, (1, s) + a_bar.shape)
    _, h = lax.associative_scan(_lin_combine, (a_el, bu), axis=1)
    c_mat = lax.complex(c_re.astype(f32), c_im.astype(f32))
    y = jnp.real(jnp.einsum('bsgp,ghp->bsgh', h, c_mat)) + d_skip.astype(f32).reshape(SSM_GROUPS, SSM_GROUP) * u32
    y = jax.nn.gelu(y.reshape(bsz, s, SSM_WIDTH))
    y = y * jax.nn.sigmoid(y @ w_glu.astype(f32))
    h_last = h[:, -1]
    return (y.astype(u.dtype), jnp.real(h_last).astype(h0_re.dtype), jnp.imag(h_last).astype(h0_re.dtype))


def _mem_kv(mem, g_mem, w_k, w_v):
    b, m, _ = mem.shape
    mn = _rmsnorm(mem, g_mem)
    k = (mn @ w_k).reshape(b, m, MEM_HEADS, MEM_HEAD_DIM)
    v = (mn @ w_v).reshape(b, m, MEM_HEADS, MEM_HEAD_DIM)
    return k, v


def _mem_attend(xn, mem_k, mem_v, w_q, w_o):
    b, s, _ = xn.shape
    q = (xn @ w_q).reshape(b, s, MEM_HEADS, MEM_HEAD_DIM)
    scores = jnp.einsum('bshd,bmhd->bhsm', q, mem_k).astype(jnp.float32) / math.sqrt(MEM_HEAD_DIM)
    probs = jax.nn.softmax(scores, axis=-1).astype(xn.dtype)
    o = jnp.einsum('bhsm,bmhd->bshd', probs, mem_v).reshape(b, s, D_MODEL)
    return o @ w_o


def _layer(x, mem_k, mem_v, conv_state, h_re, h_im,
           g_mix, w_in, conv_w, a_re, a_im, log_dt, b_re, b_im, c_re, c_im, d_skip, w_glu,
           g_grp_a, g_grp_b, w_out, g_xattn, w_q, w_o, g_mlp, w_up, w_down):
    xn = _rmsnorm(x, g_mix)
    proj = xn @ w_in
    b_gate, c_gate, v, u = jnp.split(proj, [CONV_CH, 2 * CONV_CH, 3 * CONV_CH], axis=-1)
    y_a, new_conv = _short_conv(b_gate, c_gate, v, conv_state, conv_w)
    y_b, new_re, new_im = _s5(u, h_re, h_im, a_re, a_im, log_dt, b_re, b_im, c_re, c_im, d_skip, w_glu)
    y = jnp.concatenate([_rmsnorm(y_a, g_grp_a), _rmsnorm(y_b, g_grp_b)], axis=-1)
    x = x + y @ w_out
    x = x + _mem_attend(_rmsnorm(x, g_xattn), mem_k, mem_v, w_q, w_o)
    hdn = jnp.square(jax.nn.relu(_rmsnorm(x, g_mlp) @ w_up))
    x = x + hdn @ w_down
    return x, new_conv, new_re, new_im


def setup_inputs(seed: int = 0) -> dict:
    key = jax.random.key(seed)
    ks = jax.random.split(key, 40)
    f32 = jnp.float32

    def nrm(k, shape, fan_in):
        return jax.random.normal(k, shape, f32) * fan_in ** -0.5

    def gain(k, shape):
        return 1.0 + 0.02 * jax.random.normal(k, shape, f32)

    L, G, P, H = DEPTH, SSM_GROUPS, SSM_STATE, SSM_GROUP
    a_im_base = jnp.pi * jnp.arange(P, dtype=f32)
    return {
        "x_prompt": jax.random.normal(ks[0], (BATCH, SEQ, D_MODEL), f32),
        "x_sample": jax.random.normal(ks[1], (DEC_BATCH, DEC_SEQ, D_MODEL), f32),
        "mem_prompt": jax.random.normal(ks[2], (BATCH, N_MEM, D_MODEL), f32),
        "state_conv": jax.random.normal(ks[3], (L, DEC_BATCH, CONV_K - 1, CONV_CH), f32),
        "state_ssm_re": 0.5 * jax.random.normal(ks[4], (L, DEC_BATCH, G, P), f32),
        "state_ssm_im": 0.5 * jax.random.normal(ks[5], (L, DEC_BATCH, G, P), f32),
        "cache_mem_k": jax.random.normal(ks[6], (L, DEC_BATCH, N_MEM, MEM_HEADS, MEM_HEAD_DIM), f32),
        "cache_mem_v": jax.random.normal(ks[7], (L, DEC_BATCH, N_MEM, MEM_HEADS, MEM_HEAD_DIM), f32),
        "g_mix": gain(ks[8], (L, D_MODEL)),
        "w_in": nrm(ks[9], (L, D_MODEL, IN_PROJ_WIDTH), D_MODEL),
        "conv_w": nrm(ks[10], (L, CONV_K, CONV_CH), CONV_K),
        "ssm_a_re": -0.5 + 0.01 * jax.random.normal(ks[11], (L, G, P), f32),
        "ssm_a_im": a_im_base + 0.01 * jax.random.normal(ks[12], (L, G, P), f32),
        "ssm_log_dt": jax.random.uniform(ks[13], (L, G), f32, math.log(DT_MIN), math.log(DT_MAX)),
        "ssm_b_re": nrm(ks[14], (L, G, P, H), 2 * H),
        "ssm_b_im": nrm(ks[15], (L, G, P, H), 2 * H),
        "ssm_c_re": nrm(ks[16], (L, G, H, P), 2 * P),
        "ssm_c_im": nrm(ks[17], (L, G, H, P), 2 * P),
        "ssm_d": jax.random.normal(ks[18], (L, SSM_WIDTH), f32),
        "w_glu": nrm(ks[19], (L, SSM_WIDTH, SSM_WIDTH), SSM_WIDTH),
        "g_grp_a": gain(ks[20], (L, CONV_CH)),
        "g_grp_b": gain(ks[21], (L, SSM_WIDTH)),
        "w_out": nrm(ks[22], (L, MIX_WIDTH, D_MODEL), MIX_WIDTH),
        "g_xattn": gain(ks[23], (L, D_MODEL)),
        "g_mem": gain(ks[24], (L, D_MODEL)),
        "w_q": nrm(ks[25], (L, D_MODEL, D_MODEL), D_MODEL),
        "w_k": nrm(ks[26], (L, D_MODEL, D_MODEL), D_MODEL),
        "w_v": nrm(ks[27], (L, D_MODEL, D_MODEL), D_MODEL),
        "w_o": nrm(ks[28], (L, D_MODEL, D_MODEL), D_MODEL),
        "g_mlp": gain(ks[29], (L, D_MODEL)),
        "w_up": nrm(ks[30], (L, D_MODEL, D_FF), D_MODEL),
        "w_down": nrm(ks[31], (L, D_FF, D_MODEL), D_FF),
        "g_final": gain(ks[32], (D_MODEL,)),
    }


def reference(x_prompt, x_sample, mem_prompt, state_conv, state_ssm_re, state_ssm_im,
              cache_mem_k, cache_mem_v,
              g_mix, w_in, conv_w, ssm_a_re, ssm_a_im, ssm_log_dt, ssm_b_re, ssm_b_im,
              ssm_c_re, ssm_c_im, ssm_d, w_glu, g_grp_a, g_grp_b, w_out,
              g_xattn, g_mem, w_q, w_k, w_v, w_o, g_mlp, w_up, w_down, g_final):
    bp = x_prompt.shape[0]
    xp = x_prompt
    xs = x_sample
    conv_p, re_p, im_p, mk_p, mv_p = [], [], [], [], []
    conv_s, re_s, im_s = [], [], []
    zero_conv = jnp.zeros((bp, CONV_K - 1, CONV_CH), x_prompt.dtype)
    zero_h = jnp.zeros((bp, SSM_GROUPS, SSM_STATE), x_prompt.dtype)
    for l in range(DEPTH):
        lw = (g_mix[l], w_in[l], conv_w[l], ssm_a_re[l], ssm_a_im[l], ssm_log_dt[l],
              ssm_b_re[l], ssm_b_im[l], ssm_c_re[l], ssm_c_im[l], ssm_d[l], w_glu[l],
              g_grp_a[l], g_grp_b[l], w_out[l], g_xattn[l], w_q[l], w_o[l],
              g_mlp[l], w_up[l], w_down[l])
        mk, mv = _mem_kv(mem_prompt, g_mem[l], w_k[l], w_v[l])
        xp, c_new, r_new, i_new = _layer(xp, mk, mv, zero_conv, zero_h, zero_h, *lw)
        conv_p.append(c_new)
        re_p.append(r_new)
        im_p.append(i_new)
        mk_p.append(mk)
        mv_p.append(mv)
        xs, c_new, r_new, i_new = _layer(xs, cache_mem_k[l], cache_mem_v[l], state_conv[l],
                                         state_ssm_re[l], state_ssm_im[l], *lw)
        conv_s.append(c_new)
        re_s.append(r_new)
        im_s.append(i_new)
    y_prompt = _rmsnorm(xp, g_final)
    y_sample = _rmsnorm(xs, g_final)
    return (y_prompt, y_sample,
            jnp.stack(conv_p), jnp.stack(re_p), jnp.stack(im_p), jnp.stack(mk_p), jnp.stack(mv_p),
            jnp.stack(conv_s), jnp.stack(re_s), jnp.stack(im_s))
```

```python
import functools

import jax
import jax.numpy as jnp
from jax import lax
from jax.experimental import pallas as pl
from jax.experimental.pallas import tpu as pltpu

D_MODEL = 1024
CONV_CH = 512
CONV_K = 3
SSM_WIDTH = 512
SSM_GROUP = 16
SSM_GROUPS = 32
SSM_STATE = 64
N_MEM = 256
MEM_HEADS = 4
MEM_HEAD_DIM = D_MODEL // MEM_HEADS
D_FF = 4 * D_MODEL
IN_PROJ_WIDTH = 3 * CONV_CH + SSM_WIDTH
EPS = 1e-6
A_RE_MAX = -1e-4

F32 = jnp.float32
BF16 = jnp.bfloat16

V7X_F32_SUBLANES = 8
V7X_MXU_DIM = 256
V7X_VMEM_BYTES = 64 * 1024 * 1024

HALF_GROUPS = V7X_MXU_DIM // SSM_GROUP
HALF_STATE = HALF_GROUPS * SSM_STATE
N_HALVES = SSM_GROUPS // HALF_GROUPS
STATE_LANES = 2 * SSM_GROUPS * SSM_STATE

BATCH_TILE = V7X_F32_SUBLANES
SEQ_TILE = 64
ROW_TILE = 512
MEMKV_BATCH_TILE = 2
VMEM_LIMIT = V7X_VMEM_BYTES - 8 * 1024 * 1024


def _rmsnorm(x, g):
    inv = lax.rsqrt(jnp.mean(x * x, axis=-1, keepdims=True) + EPS)
    return x * inv * g


def _dot(a, b):
    return jnp.dot(a, b, preferred_element_type=F32)


def _resident(block_shape, index_map):
    return pl.BlockSpec(block_shape, index_map, pipeline_mode=pl.Buffered(1))


def _mixer_body(x_ref, conv0_ref, h0_ref, gmix_ref, win_ref, convw_ref, ar_ref, ai_ref, bw_ref, cw_ref,
                dskip_ref, wglu_ref, ga_ref, gb_ref, wout_ref,
                o_ref, convo_ref, ho_ref,
                bu_scr, conv_scr, h_scr, *, tseq):
    rows = BATCH_TILE * tseq

    @pl.when(pl.program_id(1) == 0)
    def _():
        conv_scr[...] = conv0_ref[...].reshape((CONV_K - 1) * BATCH_TILE, CONV_CH)
        h_scr[...] = h0_ref[...]

    x = pltpu.einshape("btd->tbd", x_ref[...]).reshape(rows, D_MODEL)
    xn = _rmsnorm(x, gmix_ref[...]).astype(BF16)
    proj = _dot(xn, win_ref[...])
    b_gate = proj[:, 0:CONV_CH]
    c_gate = proj[:, CONV_CH:2 * CONV_CH]
    v = proj[:, 2 * CONV_CH:3 * CONV_CH]
    u = proj[:, 3 * CONV_CH:]

    xin = c_gate * v
    xp = jnp.concatenate([conv_scr[...], xin], axis=0)
    w = convw_ref[...]
    conv = w[0:1] * xp[0:rows]
    for k in range(1, CONV_K):
        conv = conv + w[k:k + 1] * xp[k * BATCH_TILE:k * BATCH_TILE + rows]
    conv_scr[...] = xp[rows:]
    y_a = _rmsnorm(b_gate * conv, ga_ref[...]).astype(BF16)

    u_bf = u.astype(BF16)
    for hf in range(N_HALVES):
        bu_scr[:, hf * 2 * HALF_STATE:(hf + 1) * 2 * HALF_STATE] = _dot(
            u_bf[:, hf * V7X_MXU_DIM:(hf + 1) * V7X_MXU_DIM], bw_ref[hf])
    for hf in range(N_HALVES):
        re_lanes = pl.ds(hf * 2 * HALF_STATE, HALF_STATE)
        im_lanes = pl.ds(hf * 2 * HALF_STATE + HALF_STATE, HALF_STATE)
        a_re = jnp.broadcast_to(ar_ref[hf:hf + 1, :], (BATCH_TILE, HALF_STATE))
        a_im = jnp.broadcast_to(ai_ref[hf:hf + 1, :], (BATCH_TILE, HALF_STATE))

        def step(t, carry, re_lanes=re_lanes, im_lanes=im_lanes, a_re=a_re, a_im=a_im):
            h_re, h_im = carry
            tok = pl.ds(pl.multiple_of(t * BATCH_TILE, BATCH_TILE), BATCH_TILE)
            n_re = a_re * h_re - a_im * h_im + bu_scr[tok, re_lanes]
            n_im = a_re * h_im + a_im * h_re + bu_scr[tok, im_lanes]
            bu_scr[tok, re_lanes] = n_re
            bu_scr[tok, im_lanes] = n_im
            return n_re, n_im

        h_re, h_im = lax.fori_loop(0, tseq, step, (h_scr[:, re_lanes], h_scr[:, im_lanes]), unroll=2)
        h_scr[:, re_lanes] = h_re
        h_scr[:, im_lanes] = h_im
    y_s = jnp.concatenate(
        [_dot(bu_scr[:, hf * 2 * HALF_STATE:(hf + 1) * 2 * HALF_STATE].astype(BF16), cw_ref[hf])
         for hf in range(N_HALVES)], axis=1) + dskip_ref[...] * u
    y_s = jax.nn.gelu(y_s)
    y_s = y_s * jax.nn.sigmoid(_dot(y_s.astype(BF16), wglu_ref[...]))
    y_b = _rmsnorm(y_s, gb_ref[...]).astype(BF16)

    out = x + _dot(jnp.concatenate([y_a, y_b], axis=1), wout_ref[...])
    o_ref[...] = pltpu.einshape("tbd->btd", out.reshape(tseq, BATCH_TILE, D_MODEL))
    convo_ref[...] = conv_scr[...].reshape(CONV_K - 1, BATCH_TILE, CONV_CH)
    ho_ref[...] = h_scr[...]


def _mixer(x, conv0, h0, lw, layer):
    batch, seq, _ = x.shape
    tseq = min(SEQ_TILE, seq)
    assert batch % BATCH_TILE == 0 and seq % tseq == 0
    rows = BATCH_TILE * tseq

    def layer_block(*shape):
        return _resident((None,) + shape, lambda g, s: (layer,) + (0,) * len(shape))

    return pl.pallas_call(
        functools.partial(_mixer_body, tseq=tseq),
        grid=(batch // BATCH_TILE, seq // tseq),
        in_specs=[
            pl.BlockSpec((BATCH_TILE, tseq, D_MODEL), lambda g, s: (g, s, 0)),
            pl.BlockSpec((CONV_K - 1, BATCH_TILE, CONV_CH), lambda g, s: (0, g, 0)),
            pl.BlockSpec((BATCH_TILE, STATE_LANES), lambda g, s: (g, 0)),
            layer_block(1, D_MODEL),
            layer_block(D_MODEL, IN_PROJ_WIDTH),
            layer_block(CONV_K, CONV_CH),
            layer_block(N_HALVES, HALF_STATE),
            layer_block(N_HALVES, HALF_STATE),
            layer_block(N_HALVES, V7X_MXU_DIM, 2 * HALF_STATE),
            layer_block(N_HALVES, 2 * HALF_STATE, V7X_MXU_DIM),
            layer_block(1, SSM_WIDTH),
            layer_block(SSM_WIDTH, SSM_WIDTH),
            layer_block(1, CONV_CH),
            layer_block(1, SSM_WIDTH),
            layer_block(D_MODEL, D_MODEL),
        ],
        out_specs=[
            pl.BlockSpec((BATCH_TILE, tseq, D_MODEL), lambda g, s: (g, s, 0)),
            pl.BlockSpec((CONV_K - 1, BATCH_TILE, CONV_CH), lambda g, s: (0, g, 0)),
            pl.BlockSpec((BATCH_TILE, STATE_LANES), lambda g, s: (g, 0)),
        ],
        out_shape=[
            jax.ShapeDtypeStruct(x.shape, F32),
            jax.ShapeDtypeStruct(conv0.shape, F32),
            jax.ShapeDtypeStruct(h0.shape, F32),
        ],
        scratch_shapes=[
            pltpu.VMEM((rows, STATE_LANES), F32),
            pltpu.VMEM(((CONV_K - 1) * BATCH_TILE, CONV_CH), F32),
            pltpu.VMEM((BATCH_TILE, STATE_LANES), F32),
        ],
        compiler_params=pltpu.CompilerParams(
            dimension_semantics=("arbitrary", "arbitrary"), vmem_limit_bytes=VMEM_LIMIT),
        name=f"mixer_l{layer}_s{seq}",
    )(x, conv0, h0, lw["g_mix"], lw["w_in"], lw["conv_w"], lw["a_re"], lw["a_im"], lw["b_blk"], lw["c_blk"],
      lw["d_skip"], lw["w_glu"], lw["g_grp_a"], lw["g_grp_b"], lw["w_out"])


def _attend_body(x_ref, k_ref, v_ref, g_ref, wq_ref, wo_ref, o_ref):
    x = x_ref[...]
    xn = _rmsnorm(x, g_ref[...]).astype(BF16)
    q = (_dot(xn, wq_ref[...]) * (MEM_HEAD_DIM ** -0.5)).astype(BF16)
    k = k_ref[...].astype(BF16)
    v = v_ref[...].astype(BF16)
    heads = []
    for h in range(MEM_HEADS):
        lanes = slice(h * MEM_HEAD_DIM, (h + 1) * MEM_HEAD_DIM)
        s = lax.dot_general(q[:, lanes], k[:, lanes], (((1,), (1,)), ((), ())), preferred_element_type=F32)
        p = jnp.exp(s - jnp.max(s, axis=-1, keepdims=True))
        o = _dot(p.astype(BF16), v[:, lanes]) / jnp.sum(p, axis=-1, keepdims=True)
        heads.append(o.astype(BF16))
    o_ref[...] = x + _dot(jnp.concatenate(heads, axis=1), wo_ref[...])


def _attend(x, mem_k, mem_v, lw, layer):
    batch, seq, _ = x.shape
    tq = min(ROW_TILE, seq)
    assert seq % tq == 0
    kv_spec = pl.BlockSpec((None, None, N_MEM, D_MODEL), lambda b, s: (layer, b, 0, 0))
    return pl.pallas_call(
        _attend_body,
        grid=(batch, seq // tq),
        in_specs=[
            pl.BlockSpec((None, tq, D_MODEL), lambda b, s: (b, s, 0)),
            kv_spec,
            kv_spec,
            _resident((None, 1, D_MODEL), lambda b, s: (layer, 0, 0)),
            _resident((None, D_MODEL, D_MODEL), lambda b, s: (layer, 0, 0)),
            _resident((None, D_MODEL, D_MODEL), lambda b, s: (layer, 0, 0)),
        ],
        out_specs=pl.BlockSpec((None, tq, D_MODEL), lambda b, s: (b, s, 0)),
        out_shape=jax.ShapeDtypeStruct(x.shape, F32),
        compiler_params=pltpu.CompilerParams(
            dimension_semantics=("arbitrary", "arbitrary"), vmem_limit_bytes=VMEM_LIMIT),
        name=f"attend_l{layer}_s{seq}",
    )(x, mem_k, mem_v, lw["g_xattn"], lw["w_q"], lw["w_o"])


def _mlp_body(x_ref, g_ref, wup_ref, wdown_ref, gfinal_ref, o_ref, *, final_norm):
    x = x_ref[...]
    xn = _rmsnorm(x, g_ref[...]).astype(BF16)
    hdn = jnp.square(jnp.maximum(_dot(xn, wup_ref[...]), 0.0)).astype(BF16)
    out = x + _dot(hdn, wdown_ref[...])
    if final_norm:
        out = _rmsnorm(out, gfinal_ref[...])
    o_ref[...] = out


def _mlp(x, lw, g_final, layer, final_norm):
    shape = x.shape
    x2 = x.reshape(-1, D_MODEL)
    n_rows = x2.shape[0]
    tm = min(ROW_TILE, n_rows)
    assert n_rows % tm == 0
    out = pl.pallas_call(
        functools.partial(_mlp_body, final_norm=final_norm),
        grid=(n_rows // tm,),
        in_specs=[
            pl.BlockSpec((tm, D_MODEL), lambda i: (i, 0)),
            _resident((None, 1, D_MODEL), lambda i: (layer, 0, 0)),
            _resident((None, D_MODEL, D_FF), lambda i: (layer, 0, 0)),
            _resident((None, D_FF, D_MODEL), lambda i: (layer, 0, 0)),
            _resident((1, D_MODEL), lambda i: (0, 0)),
        ],
        out_specs=pl.BlockSpec((tm, D_MODEL), lambda i: (i, 0)),
        out_shape=jax.ShapeDtypeStruct(x2.shape, F32),
        compiler_params=pltpu.CompilerParams(
            dimension_semantics=("arbitrary",), vmem_limit_bytes=VMEM_LIMIT),
        name=f"mlp_l{layer}_r{n_rows}",
    )(x2, lw["g_mlp"], lw["w_up"], lw["w_down"], g_final)
    return out.reshape(shape)


def _memkv_body(mem_ref, g_ref, wk_ref, wv_ref, k_ref, v_ref):
    mn = _rmsnorm(mem_ref[...].reshape(MEMKV_BATCH_TILE * N_MEM, D_MODEL), g_ref[...]).astype(BF16)
    k_ref[...] = _dot(mn, wk_ref[...]).reshape(MEMKV_BATCH_TILE, N_MEM, D_MODEL)
    v_ref[...] = _dot(mn, wv_ref[...]).reshape(MEMKV_BATCH_TILE, N_MEM, D_MODEL)


def _memkv(mem, g_mem, w_k, w_v):
    batch = mem.shape[0]
    depth = w_k.shape[0]
    assert batch % MEMKV_BATCH_TILE == 0
    kv_shape = jax.ShapeDtypeStruct((depth, batch, N_MEM, D_MODEL), F32)
    kv_spec = pl.BlockSpec((None, MEMKV_BATCH_TILE, N_MEM, D_MODEL), lambda l, b: (l, b, 0, 0))
    w_spec = pl.BlockSpec((None, D_MODEL, D_MODEL), lambda l, b: (l, 0, 0))
    return pl.pallas_call(
        _memkv_body,
        grid=(depth, batch // MEMKV_BATCH_TILE),
        in_specs=[
            pl.BlockSpec((MEMKV_BATCH_TILE, N_MEM, D_MODEL), lambda l, b: (b, 0, 0)),
            pl.BlockSpec((None, 1, D_MODEL), lambda l, b: (l, 0, 0)),
            w_spec,
            w_spec,
        ],
        out_specs=[kv_spec, kv_spec],
        out_shape=[kv_shape, kv_shape],
        compiler_params=pltpu.CompilerParams(
            dimension_semantics=("arbitrary", "arbitrary"), vmem_limit_bytes=VMEM_LIMIT),
        name="memkv",
    )(mem, g_mem, w_k, w_v)


def _block_diag_halves(m):
    _, r, c = m.shape
    m = m.reshape(N_HALVES, HALF_GROUPS, r, c)
    eye = jnp.eye(HALF_GROUPS, dtype=m.dtype)
    return jnp.einsum("ngrc,gk->ngrkc", m, eye).reshape(N_HALVES, HALF_GROUPS * r, HALF_GROUPS * c)


def _s5_params(a_re, a_im, log_dt, b_re, b_im, c_re, c_im):
    lam = lax.complex(jnp.minimum(a_re, A_RE_MAX), a_im)
    dt = jnp.exp(log_dt)[..., None]
    a_bar = jnp.exp(lam * dt)
    b_bar = ((a_bar - 1.0) / lam)[..., None] * lax.complex(b_re, b_im)
    depth = a_re.shape[0]
    a_re_l = jnp.real(a_bar).reshape(depth, N_HALVES, HALF_STATE)
    a_im_l = jnp.imag(a_bar).reshape(depth, N_HALVES, HALF_STATE)
    b_hp = jnp.swapaxes(b_bar, -1, -2)
    b_blk = jnp.stack([jnp.concatenate([_block_diag_halves(jnp.real(b_hp[l])),
                                        _block_diag_halves(jnp.imag(b_hp[l]))], axis=-1)
                       for l in range(depth)])
    c_ph_re = jnp.swapaxes(c_re, -1, -2)
    c_ph_im = jnp.swapaxes(c_im, -1, -2)
    c_blk = jnp.stack([jnp.concatenate([_block_diag_halves(c_ph_re[l]),
                                        _block_diag_halves(-c_ph_im[l])], axis=-2)
                       for l in range(depth)])
    return a_re_l, a_im_l, b_blk.astype(BF16), c_blk.astype(BF16)


def _state_to_lanes(h_re, h_im):
    b = h_re.shape[0]
    parts = [h.reshape(b, N_HALVES, 1, HALF_STATE) for h in (h_re, h_im)]
    return jnp.concatenate(parts, axis=2).reshape(b, STATE_LANES)


def _lanes_to_state(h):
    b = h.shape[0]
    h = h.reshape(b, N_HALVES, 2, HALF_STATE)
    return (h[:, :, 0].reshape(b, SSM_GROUPS, SSM_STATE), h[:, :, 1].reshape(b, SSM_GROUPS, SSM_STATE))


def kernel(x_prompt, x_sample, mem_prompt, state_conv, state_ssm_re, state_ssm_im, cache_mem_k, cache_mem_v, g_mix, w_in, conv_w, ssm_a_re, ssm_a_im, ssm_log_dt, ssm_b_re, ssm_b_im, ssm_c_re, ssm_c_im, ssm_d, w_glu, g_grp_a, g_grp_b, w_out, g_xattn, g_mem, w_q, w_k, w_v, w_o, g_mlp, w_up, w_down, g_final):
    depth = w_in.shape[0]
    bp = x_prompt.shape[0]
    bs = x_sample.shape[0]
    a_re_l, a_im_l, b_blk, c_blk = _s5_params(ssm_a_re, ssm_a_im, ssm_log_dt, ssm_b_re, ssm_b_im, ssm_c_re, ssm_c_im)
    row = lambda g: g.reshape(depth, 1, -1)
    lw = {
        "g_mix": row(g_mix), "w_in": w_in.astype(BF16), "conv_w": conv_w, "a_re": a_re_l, "a_im": a_im_l,
        "b_blk": b_blk, "c_blk": c_blk, "d_skip": row(ssm_d), "w_glu": w_glu.astype(BF16),
        "g_grp_a": row(g_grp_a), "g_grp_b": row(g_grp_b), "w_out": w_out.astype(BF16),
        "g_xattn": row(g_xattn), "w_q": w_q.astype(BF16), "w_o": w_o.astype(BF16),
        "g_mlp": row(g_mlp), "w_up": w_up.astype(BF16), "w_down": w_down.astype(BF16),
    }
    g_final2 = g_final.reshape(1, D_MODEL)

    mk_p, mv_p = _memkv(mem_prompt, row(g_mem), w_k.astype(BF16), w_v.astype(BF16))
    mk_s = cache_mem_k.reshape(depth, bs, N_MEM, D_MODEL)
    mv_s = cache_mem_v.reshape(depth, bs, N_MEM, D_MODEL)

    zero_conv = jnp.zeros((CONV_K - 1, bp, CONV_CH), F32)
    zero_h = jnp.zeros((bp, STATE_LANES), F32)

    xp, xs = x_prompt, x_sample
    conv_p, re_p, im_p, conv_s, re_s, im_s = [], [], [], [], [], []
    for l in range(depth):
        last = l == depth - 1
        xp, c_new, h_new = _mixer(xp, zero_conv, zero_h, lw, l)
        conv_p.append(jnp.swapaxes(c_new, 0, 1))
        r_new, i_new = _lanes_to_state(h_new)
        re_p.append(r_new)
        im_p.append(i_new)
        xp = _attend(xp, mk_p, mv_p, lw, l)
        xp = _mlp(xp, lw, g_final2, l, last)

        xs, c_new, h_new = _mixer(xs, jnp.swapaxes(state_conv[l], 0, 1),
                                  _state_to_lanes(state_ssm_re[l], state_ssm_im[l]), lw, l)
        conv_s.append(jnp.swapaxes(c_new, 0, 1))
        r_new, i_new = _lanes_to_state(h_new)
        re_s.append(r_new)
        im_s.append(i_new)
        xs = _attend(xs, mk_s, mv_s, lw, l)
        xs = _mlp(xs, lw, g_final2, l, last)

    kv_shape = (depth, bp, N_MEM, MEM_HEADS, MEM_HEAD_DIM)
    return (xp, xs,
            jnp.stack(conv_p), jnp.stack(re_p), jnp.stack(im_p), mk_p.reshape(kv_shape), mv_p.reshape(kv_shape),
            jnp.stack(conv_s), jnp.stack(re_s), jnp.stack(im_s))
```

```python
import functools

import jax
import jax.numpy as jnp
from jax import lax
from jax.experimental import pallas as pl
from jax.experimental.pallas import tpu as pltpu

D_MODEL = 1024
CONV_CH = 512
CONV_K = 3
SSM_WIDTH = 512
SSM_GROUP = 16
SSM_GROUPS = 32
SSM_STATE = 64
N_MEM = 256
MEM_HEADS = 4
MEM_HEAD_DIM = D_MODEL // MEM_HEADS
D_FF = 4 * D_MODEL
IN_PROJ_WIDTH = 3 * CONV_CH + SSM_WIDTH
EPS = 1e-6
A_RE_MAX = -1e-4

F32 = jnp.float32
BF16 = jnp.bfloat16

V7X_F32_SUBLANES = 8
V7X_MXU_DIM = 256
V7X_VMEM_BYTES = 64 * 1024 * 1024

PAIR_GROUPS = V7X_MXU_DIM // (2 * SSM_STATE)
PAIR_U = PAIR_GROUPS * SSM_GROUP
PAIR_STATE = PAIR_GROUPS * SSM_STATE
N_PAIRS = SSM_GROUPS // PAIR_GROUPS
CHUNK = V7X_MXU_DIM // PAIR_U
SSM_LANES = SSM_GROUPS * SSM_STATE

BATCH_TILE = V7X_F32_SUBLANES
SEQ_TILE = 64
ROW_TILE = 512
MEMKV_BATCH_TILE = 2
VMEM_LIMIT = V7X_VMEM_BYTES - 8 * 1024 * 1024


def _rmsnorm(x, g):
    inv = lax.rsqrt(jnp.mean(x * x, axis=-1, keepdims=True) + EPS)
    return x * inv * g


def _dot(a, b):
    return jnp.dot(a, b, preferred_element_type=F32)


def _resident(block_shape, index_map):
    return pl.BlockSpec(block_shape, index_map, pipeline_mode=pl.Buffered(1))


def _s5_chunked(u, hre_scr, him_scr, a8r_ref, a8i_ref, w1_ref, w2_ref, n_chunks):
    crow = n_chunks * BATCH_TILE
    u_pos = u.reshape(n_chunks, CHUNK, BATCH_TILE, SSM_WIDTH)
    u_pos = [u_pos[:, p].reshape(crow, SSM_WIDTH) for p in range(CHUNK)]
    z_re, z_im, y_in = [], [], []
    for gp in range(N_PAIRS):
        lanes = slice(gp * PAIR_U, (gp + 1) * PAIR_U)
        xg = jnp.concatenate([up[:, lanes] for up in u_pos], axis=1).astype(BF16)
        zy = _dot(xg, w1_ref[gp])
        z_re.append(zy[:, 0:PAIR_STATE])
        z_im.append(zy[:, PAIR_STATE:2 * PAIR_STATE])
        y_in.append(zy[:, 2 * PAIR_STATE:])
    z_re = jnp.concatenate(z_re, axis=1)
    z_im = jnp.concatenate(z_im, axis=1)

    a_re = jnp.broadcast_to(a8r_ref[...], (BATCH_TILE, SSM_LANES))
    a_im = jnp.broadcast_to(a8i_ref[...], (BATCH_TILE, SSM_LANES))
    h_re, h_im = hre_scr[...], him_scr[...]
    prev_re, prev_im = [], []
    for c in range(n_chunks):
        prev_re.append(h_re)
        prev_im.append(h_im)
        seqs = slice(c * BATCH_TILE, (c + 1) * BATCH_TILE)
        h_re, h_im = (a_re * h_re - a_im * h_im + z_re[seqs], a_re * h_im + a_im * h_re + z_im[seqs])
    hre_scr[...] = h_re
    him_scr[...] = h_im
    prev_re = jnp.concatenate(prev_re, axis=0)
    prev_im = jnp.concatenate(prev_im, axis=0)

    y_pair = []
    for gp in range(N_PAIRS):
        lanes = slice(gp * PAIR_STATE, (gp + 1) * PAIR_STATE)
        sp = jnp.concatenate([prev_re[:, lanes], prev_im[:, lanes]], axis=1).astype(BF16)
        y_pair.append(y_in[gp] + _dot(sp, w2_ref[gp]))
    y_pos = [jnp.concatenate([yp[:, q * PAIR_U:(q + 1) * PAIR_U] for yp in y_pair], axis=1)
             .reshape(n_chunks, BATCH_TILE, SSM_WIDTH) for q in range(CHUNK)]
    return jnp.stack(y_pos, axis=1).reshape(n_chunks * CHUNK * BATCH_TILE, SSM_WIDTH)


def _mixer_body(x_ref, conv0_ref, hre0_ref, him0_ref, gmix_ref, win_ref, convw_ref, a8r_ref, a8i_ref, w1_ref, w2_ref,
                dskip_ref, wglu_ref, ga_ref, gb_ref, wout_ref,
                o_ref, convo_ref, hreo_ref, himo_ref,
                conv_scr, hre_scr, him_scr, *, tseq):
    rows = BATCH_TILE * tseq

    @pl.when(pl.program_id(1) == 0)
    def _():
        conv_scr[...] = conv0_ref[...].reshape((CONV_K - 1) * BATCH_TILE, CONV_CH)
        hre_scr[...] = hre0_ref[...]
        him_scr[...] = him0_ref[...]

    x = pltpu.einshape("btd->tbd", x_ref[...]).reshape(rows, D_MODEL)
    xn = _rmsnorm(x, gmix_ref[...]).astype(BF16)
    proj = _dot(xn, win_ref[...])
    b_gate = proj[:, 0:CONV_CH]
    c_gate = proj[:, CONV_CH:2 * CONV_CH]
    v = proj[:, 2 * CONV_CH:3 * CONV_CH]
    u = proj[:, 3 * CONV_CH:]

    xin = c_gate * v
    xp = jnp.concatenate([conv_scr[...], xin], axis=0)
    w = convw_ref[...]
    conv = w[0:1] * xp[0:rows]
    for k in range(1, CONV_K):
        conv = conv + w[k:k + 1] * xp[k * BATCH_TILE:k * BATCH_TILE + rows]
    conv_scr[...] = xp[rows:]
    y_a = _rmsnorm(b_gate * conv, ga_ref[...]).astype(BF16)

    y_s = _s5_chunked(u, hre_scr, him_scr, a8r_ref, a8i_ref, w1_ref, w2_ref, tseq // CHUNK) + dskip_ref[...] * u
    y_s = jax.nn.gelu(y_s)
    y_s = y_s * jax.nn.sigmoid(_dot(y_s.astype(BF16), wglu_ref[...]))
    y_b = _rmsnorm(y_s, gb_ref[...]).astype(BF16)

    out = x + _dot(jnp.concatenate([y_a, y_b], axis=1), wout_ref[...])
    o_ref[...] = pltpu.einshape("tbd->btd", out.reshape(tseq, BATCH_TILE, D_MODEL))
    convo_ref[...] = conv_scr[...].reshape(CONV_K - 1, BATCH_TILE, CONV_CH)
    hreo_ref[...] = hre_scr[...]
    himo_ref[...] = him_scr[...]


def _mixer(x, conv0, h0_re, h0_im, lw, layer):
    batch, seq, _ = x.shape
    tseq = min(SEQ_TILE, seq)
    assert batch % BATCH_TILE == 0 and seq % tseq == 0 and tseq % CHUNK == 0

    def layer_block(*shape):
        return _resident((None,) + shape, lambda g, s: (layer,) + (0,) * len(shape))

    state_spec = pl.BlockSpec((BATCH_TILE, SSM_LANES), lambda g, s: (g, 0))
    conv_spec = pl.BlockSpec((CONV_K - 1, BATCH_TILE, CONV_CH), lambda g, s: (0, g, 0))
    x_spec = pl.BlockSpec((BATCH_TILE, tseq, D_MODEL), lambda g, s: (g, s, 0))
    return pl.pallas_call(
        functools.partial(_mixer_body, tseq=tseq),
        grid=(batch // BATCH_TILE, seq // tseq),
        in_specs=[
            x_spec, conv_spec, state_spec, state_spec,
            layer_block(1, D_MODEL),
            layer_block(D_MODEL, IN_PROJ_WIDTH),
            layer_block(CONV_K, CONV_CH),
            layer_block(1, SSM_LANES),
            layer_block(1, SSM_LANES),
            layer_block(N_PAIRS, V7X_MXU_DIM, 2 * V7X_MXU_DIM),
            layer_block(N_PAIRS, V7X_MXU_DIM, V7X_MXU_DIM),
            layer_block(1, SSM_WIDTH),
            layer_block(SSM_WIDTH, SSM_WIDTH),
            layer_block(1, CONV_CH),
            layer_block(1, SSM_WIDTH),
            layer_block(D_MODEL, D_MODEL),
        ],
        out_specs=[x_spec, conv_spec, state_spec, state_spec],
        out_shape=[
            jax.ShapeDtypeStruct(x.shape, F32),
            jax.ShapeDtypeStruct(conv0.shape, F32),
            jax.ShapeDtypeStruct(h0_re.shape, F32),
            jax.ShapeDtypeStruct(h0_im.shape, F32),
        ],
        scratch_shapes=[
            pltpu.VMEM(((CONV_K - 1) * BATCH_TILE, CONV_CH), F32),
            pltpu.VMEM((BATCH_TILE, SSM_LANES), F32),
            pltpu.VMEM((BATCH_TILE, SSM_LANES), F32),
        ],
        compiler_params=pltpu.CompilerParams(
            dimension_semantics=("arbitrary", "arbitrary"), vmem_limit_bytes=VMEM_LIMIT),
        name=f"mixer_l{layer}_s{seq}",
    )(x, conv0, h0_re, h0_im, lw["g_mix"], lw["w_in"], lw["conv_w"], lw["a8_re"], lw["a8_im"], lw["w1"], lw["w2"],
      lw["d_skip"], lw["w_glu"], lw["g_grp_a"], lw["g_grp_b"], lw["w_out"])


def _attend_body(x_ref, k_ref, v_ref, g_ref, wq_ref, wo_ref, o_ref):
    x = x_ref[...]
    xn = _rmsnorm(x, g_ref[...]).astype(BF16)
    q = (_dot(xn, wq_ref[...]) * (MEM_HEAD_DIM ** -0.5)).astype(BF16)
    k = k_ref[...].astype(BF16)
    v = v_ref[...].astype(BF16)
    heads = []
    for h in range(MEM_HEADS):
        lanes = slice(h * MEM_HEAD_DIM, (h + 1) * MEM_HEAD_DIM)
        s = lax.dot_general(q[:, lanes], k[:, lanes], (((1,), (1,)), ((), ())), preferred_element_type=F32)
        p = jnp.exp(s - jnp.max(s, axis=-1, keepdims=True))
        o = _dot(p.astype(BF16), v[:, lanes]) / jnp.sum(p, axis=-1, keepdims=True)
        heads.append(o.astype(BF16))
    o_ref[...] = x + _dot(jnp.concatenate(heads, axis=1), wo_ref[...])


def _attend(x, mem_k, mem_v, lw, layer):
    batch, seq, _ = x.shape
    tq = min(ROW_TILE, seq)
    assert seq % tq == 0
    kv_spec = pl.BlockSpec((None, None, N_MEM, D_MODEL), lambda b, s: (layer, b, 0, 0))
    return pl.pallas_call(
        _attend_body,
        grid=(batch, seq // tq),
        in_specs=[
            pl.BlockSpec((None, tq, D_MODEL), lambda b, s: (b, s, 0)),
            kv_spec,
            kv_spec,
            _resident((None, 1, D_MODEL), lambda b, s: (layer, 0, 0)),
            _resident((None, D_MODEL, D_MODEL), lambda b, s: (layer, 0, 0)),
            _resident((None, D_MODEL, D_MODEL), lambda b, s: (layer, 0, 0)),
        ],
        out_specs=pl.BlockSpec((None, tq, D_MODEL), lambda b, s: (b, s, 0)),
        out_shape=jax.ShapeDtypeStruct(x.shape, F32),
        compiler_params=pltpu.CompilerParams(
            dimension_semantics=("arbitrary", "arbitrary"), vmem_limit_bytes=VMEM_LIMIT),
        name=f"attend_l{layer}_s{seq}",
    )(x, mem_k, mem_v, lw["g_xattn"], lw["w_q"], lw["w_o"])


def _mlp_body(x_ref, g_ref, wup_ref, wdown_ref, gfinal_ref, o_ref, *, final_norm):
    x = x_ref[...]
    xn = _rmsnorm(x, g_ref[...]).astype(BF16)
    hdn = jnp.square(jnp.maximum(_dot(xn, wup_ref[...]), 0.0)).astype(BF16)
    out = x + _dot(hdn, wdown_ref[...])
    if final_norm:
        out = _rmsnorm(out, gfinal_ref[...])
    o_ref[...] = out


def _mlp(x, lw, g_final, layer, final_norm):
    shape = x.shape
    x2 = x.reshape(-1, D_MODEL)
    n_rows = x2.shape[0]
    tm = min(ROW_TILE, n_rows)
    assert n_rows % tm == 0
    out = pl.pallas_call(
        functools.partial(_mlp_body, final_norm=final_norm),
        grid=(n_rows // tm,),
        in_specs=[
            pl.BlockSpec((tm, D_MODEL), lambda i: (i, 0)),
            _resident((None, 1, D_MODEL), lambda i: (layer, 0, 0)),
            _resident((None, D_MODEL, D_FF), lambda i: (layer, 0, 0)),
            _resident((None, D_FF, D_MODEL), lambda i: (layer, 0, 0)),
            _resident((1, D_MODEL), lambda i: (0, 0)),
        ],
        out_specs=pl.BlockSpec((tm, D_MODEL), lambda i: (i, 0)),
        out_shape=jax.ShapeDtypeStruct(x2.shape, F32),
        compiler_params=pltpu.CompilerParams(
            dimension_semantics=("arbitrary",), vmem_limit_bytes=VMEM_LIMIT),
        name=f"mlp_l{layer}_r{n_rows}",
    )(x2, lw["g_mlp"], lw["w_up"], lw["w_down"], g_final)
    return out.reshape(shape)


def _memkv_body(mem_ref, g_ref, wk_ref, wv_ref, k_ref, v_ref):
    mn = _rmsnorm(mem_ref[...].reshape(MEMKV_BATCH_TILE * N_MEM, D_MODEL), g_ref[...]).astype(BF16)
    k_ref[...] = _dot(mn, wk_ref[...]).reshape(MEMKV_BATCH_TILE, N_MEM, D_MODEL)
    v_ref[...] = _dot(mn, wv_ref[...]).reshape(MEMKV_BATCH_TILE, N_MEM, D_MODEL)


def _memkv(mem, g_mem, w_k, w_v):
    batch = mem.shape[0]
    depth = w_k.shape[0]
    assert batch % MEMKV_BATCH_TILE == 0
    kv_shape = jax.ShapeDtypeStruct((depth, batch, N_MEM, D_MODEL), F32)
    kv_spec = pl.BlockSpec((None, MEMKV_BATCH_TILE, N_MEM, D_MODEL), lambda l, b: (l, b, 0, 0))
    w_spec = pl.BlockSpec((None, D_MODEL, D_MODEL), lambda l, b: (l, 0, 0))
    return pl.pallas_call(
        _memkv_body,
        grid=(depth, batch // MEMKV_BATCH_TILE),
        in_specs=[
            pl.BlockSpec((MEMKV_BATCH_TILE, N_MEM, D_MODEL), lambda l, b: (b, 0, 0)),
            pl.BlockSpec((None, 1, D_MODEL), lambda l, b: (l, 0, 0)),
            w_spec,
            w_spec,
        ],
        out_specs=[kv_spec, kv_spec],
        out_shape=[kv_shape, kv_shape],
        compiler_params=pltpu.CompilerParams(
            dimension_semantics=("arbitrary", "arbitrary"), vmem_limit_bytes=VMEM_LIMIT),
        name="memkv",
    )(mem, g_mem, w_k, w_v)


def _pair_blocks(m):
    depth, _, r, c = m.shape
    m = m.reshape(depth, N_PAIRS, PAIR_GROUPS, r, c)
    eye = jnp.eye(PAIR_GROUPS, dtype=m.dtype)
    return jnp.einsum("lngrc,gk->lngrkc", m, eye).reshape(depth, N_PAIRS, PAIR_GROUPS * r, PAIR_GROUPS * c)


def _s5_params(a_re, a_im, log_dt, b_re, b_im, c_re, c_im):
    hi = lax.Precision.HIGHEST
    lam_re = jnp.minimum(a_re, A_RE_MAX)
    lam_im = a_im
    dt = jnp.exp(log_dt)[..., None]

    def a_pow(k):
        mag = jnp.exp(k * lam_re * dt)
        ang = k * lam_im * dt
        return mag * jnp.cos(ang), mag * jnp.sin(ang)

    pows = [a_pow(k) for k in range(CHUNK + 1)]
    a1_re, a1_im = pows[1]
    den = lam_re * lam_re + lam_im * lam_im
    f_re = ((a1_re - 1.0) * lam_re + a1_im * lam_im) / den
    f_im = (a1_im * lam_re - (a1_re - 1.0) * lam_im) / den
    bb_re = f_re[..., None] * b_re - f_im[..., None] * b_im
    bb_im = f_re[..., None] * b_im + f_im[..., None] * b_re

    def times_pow(k):
        p_re, p_im = pows[k]
        return (p_re[..., None] * bb_re - p_im[..., None] * bb_im, p_re[..., None] * bb_im + p_im[..., None] * bb_re)

    e = [times_pow(k) for k in range(CHUNK)]
    to_state = [jnp.concatenate([_pair_blocks(jnp.swapaxes(e[CHUNK - 1 - p][part], -1, -2)) for part in range(2)], axis=-1)
                for p in range(CHUNK)]
    to_state = jnp.concatenate(to_state, axis=-2)
    kern = [_pair_blocks(jnp.einsum("lghs,lgsk->lgkh", c_re, e[j][0], precision=hi)
                         - jnp.einsum("lghs,lgsk->lgkh", c_im, e[j][1], precision=hi)) for j in range(CHUNK)]
    zero = jnp.zeros_like(kern[0])
    toeplitz = jnp.concatenate(
        [jnp.concatenate([kern[q - p] if q >= p else zero for q in range(CHUNK)], axis=-1) for p in range(CHUNK)],
        axis=-2)
    w1 = jnp.concatenate([to_state, toeplitz], axis=-1)
    from_re = [_pair_blocks(jnp.swapaxes(c_re * pows[q + 1][0][:, :, None, :] - c_im * pows[q + 1][1][:, :, None, :], -1, -2))
               for q in range(CHUNK)]
    from_im = [_pair_blocks(jnp.swapaxes(-(c_re * pows[q + 1][1][:, :, None, :] + c_im * pows[q + 1][0][:, :, None, :]), -1, -2))
               for q in range(CHUNK)]
    w2 = jnp.concatenate([jnp.concatenate(from_re, axis=-1), jnp.concatenate(from_im, axis=-1)], axis=-2)
    depth = a_re.shape[0]
    a8_re, a8_im = pows[CHUNK]
    return (a8_re.reshape(depth, 1, SSM_LANES), a8_im.reshape(depth, 1, SSM_LANES), w1.astype(BF16), w2.astype(BF16))


def kernel(x_prompt, x_sample, mem_prompt, state_conv, state_ssm_re, state_ssm_im, cache_mem_k, cache_mem_v, g_mix, w_in, conv_w, ssm_a_re, ssm_a_im, ssm_log_dt, ssm_b_re, ssm_b_im, ssm_c_re, ssm_c_im, ssm_d, w_glu, g_grp_a, g_grp_b, w_out, g_xattn, g_mem, w_q, w_k, w_v, w_o, g_mlp, w_up, w_down, g_final):
    depth = w_in.shape[0]
    bp = x_prompt.shape[0]
    bs = x_sample.shape[0]
    a8_re, a8_im, w1, w2 = _s5_params(ssm_a_re, ssm_a_im, ssm_log_dt, ssm_b_re, ssm_b_im, ssm_c_re, ssm_c_im)
    row = lambda g: g.reshape(depth, 1, -1)
    lw = {
        "g_mix": row(g_mix), "w_in": w_in.astype(BF16), "conv_w": conv_w, "a8_re": a8_re, "a8_im": a8_im,
        "w1": w1, "w2": w2, "d_skip": row(ssm_d), "w_glu": w_glu.astype(BF16),
        "g_grp_a": row(g_grp_a), "g_grp_b": row(g_grp_b), "w_out": w_out.astype(BF16),
        "g_xattn": row(g_xattn), "w_q": w_q.astype(BF16), "w_o": w_o.astype(BF16),
        "g_mlp": row(g_mlp), "w_up": w_up.astype(BF16), "w_down": w_down.astype(BF16),
    }
    g_final2 = g_final.reshape(1, D_MODEL)

    mk_p, mv_p = _memkv(mem_prompt, row(g_mem), w_k.astype(BF16), w_v.astype(BF16))
    mk_s = cache_mem_k.reshape(depth, bs, N_MEM, D_MODEL)
    mv_s = cache_mem_v.reshape(depth, bs, N_MEM, D_MODEL)

    zero_conv = jnp.zeros((CONV_K - 1, bp, CONV_CH), F32)
    zero_h = jnp.zeros((bp, SSM_LANES), F32)
    state_shape = (-1, SSM_GROUPS, SSM_STATE)

    xp, xs = x_prompt, x_sample
    conv_p, re_p, im_p, conv_s, re_s, im_s = [], [], [], [], [], []
    for l in range(depth):
        last = l == depth - 1
        xp, c_new, r_new, i_new = _mixer(xp, zero_conv, zero_h, zero_h, lw, l)
        conv_p.append(jnp.swapaxes(c_new, 0, 1))
        re_p.append(r_new.reshape(state_shape))
        im_p.append(i_new.reshape(state_shape))
        xp = _attend(xp, mk_p, mv_p, lw, l)
        xp = _mlp(xp, lw, g_final2, l, last)

        xs, c_new, r_new, i_new = _mixer(xs, jnp.swapaxes(state_conv[l], 0, 1),
                                         state_ssm_re[l].reshape(bs, SSM_LANES),
                                         state_ssm_im[l].reshape(bs, SSM_LANES), lw, l)
        conv_s.append(jnp.swapaxes(c_new, 0, 1))
        re_s.append(r_new.reshape(state_shape))
        im_s.append(i_new.reshape(state_shape))
        xs = _attend(xs, mk_s, mv_s, lw, l)
        xs = _mlp(xs, lw, g_final2, l, last)

    kv_shape = (depth, bp, N_MEM, MEM_HEADS, MEM_HEAD_DIM)
    return (xp, xs,
            jnp.stack(conv_p), jnp.stack(re_p), jnp.stack(im_p), mk_p.reshape(kv_shape), mv_p.reshape(kv_shape),
            jnp.stack(conv_s), jnp.stack(re_s), jnp.stack(im_s))
```

```python
import functools

import jax
import jax.numpy as jnp
from jax import lax
from jax.experimental import pallas as pl
from jax.experimental.pallas import tpu as pltpu

D_MODEL = 1024
CONV_CH = 512
CONV_K = 3
SSM_WIDTH = 512
SSM_GROUP = 16
SSM_GROUPS = 32
SSM_STATE = 64
N_MEM = 256
MEM_HEADS = 4
MEM_HEAD_DIM = D_MODEL // MEM_HEADS
D_FF = 4 * D_MODEL
IN_PROJ_WIDTH = 3 * CONV_CH + SSM_WIDTH
EPS = 1e-6
A_RE_MAX = -1e-4

F32 = jnp.float32
BF16 = jnp.bfloat16

V7X_F32_SUBLANES = 8
V7X_MXU_DIM = 256
V7X_VMEM_BYTES = 64 * 1024 * 1024

PAIR_GROUPS = V7X_MXU_DIM // (2 * SSM_STATE)
PAIR_U = PAIR_GROUPS * SSM_GROUP
PAIR_STATE = PAIR_GROUPS * SSM_STATE
N_PAIRS = SSM_GROUPS // PAIR_GROUPS
CHUNK = V7X_MXU_DIM // PAIR_U
SSM_LANES = SSM_GROUPS * SSM_STATE

BATCH_TILE = V7X_F32_SUBLANES
SEQ_TILE = 128
ROW_TILE = 512
MEMKV_BATCH_TILE = 2
VMEM_LIMIT = V7X_VMEM_BYTES - 8 * 1024 * 1024


def _rmsnorm(x, g):
    inv = lax.rsqrt(jnp.mean(x * x, axis=-1, keepdims=True) + EPS)
    return x * inv * g


def _dot(a, b):
    return jnp.dot(a, b, preferred_element_type=F32)


def _resident(block_shape, index_map):
    return pl.BlockSpec(block_shape, index_map, pipeline_mode=pl.Buffered(1))


def _s5_chunked(u, hre_scr, him_scr, a8r_ref, a8i_ref, w1_ref, w2_ref, n_chunks):
    crow = n_chunks * BATCH_TILE
    u_pos = u.reshape(n_chunks, CHUNK, BATCH_TILE, SSM_WIDTH)
    u_pos = [u_pos[:, p].reshape(crow, SSM_WIDTH) for p in range(CHUNK)]
    z_re, z_im, y_in = [], [], []
    for gp in range(N_PAIRS):
        lanes = slice(gp * PAIR_U, (gp + 1) * PAIR_U)
        xg = jnp.concatenate([up[:, lanes] for up in u_pos], axis=1).astype(BF16)
        zy = _dot(xg, w1_ref[gp])
        z_re.append(zy[:, 0:PAIR_STATE])
        z_im.append(zy[:, PAIR_STATE:2 * PAIR_STATE])
        y_in.append(zy[:, 2 * PAIR_STATE:])
    z_re = jnp.concatenate(z_re, axis=1)
    z_im = jnp.concatenate(z_im, axis=1)

    a_re = jnp.broadcast_to(a8r_ref[...], (BATCH_TILE, SSM_LANES))
    a_im = jnp.broadcast_to(a8i_ref[...], (BATCH_TILE, SSM_LANES))
    h_re, h_im = hre_scr[...], him_scr[...]
    prev_re, prev_im = [], []
    for c in range(n_chunks):
        prev_re.append(h_re)
        prev_im.append(h_im)
        seqs = slice(c * BATCH_TILE, (c + 1) * BATCH_TILE)
        h_re, h_im = (a_re * h_re - a_im * h_im + z_re[seqs], a_re * h_im + a_im * h_re + z_im[seqs])
    hre_scr[...] = h_re
    him_scr[...] = h_im
    prev_re = jnp.concatenate(prev_re, axis=0)
    prev_im = jnp.concatenate(prev_im, axis=0)

    y_pair = []
    for gp in range(N_PAIRS):
        lanes = slice(gp * PAIR_STATE, (gp + 1) * PAIR_STATE)
        sp = jnp.concatenate([prev_re[:, lanes], prev_im[:, lanes]], axis=1).astype(BF16)
        y_pair.append(y_in[gp] + _dot(sp, w2_ref[gp]))
    y_pos = [jnp.concatenate([yp[:, q * PAIR_U:(q + 1) * PAIR_U] for yp in y_pair], axis=1)
             .reshape(n_chunks, BATCH_TILE, SSM_WIDTH) for q in range(CHUNK)]
    return jnp.stack(y_pos, axis=1).reshape(n_chunks * CHUNK * BATCH_TILE, SSM_WIDTH)


def _mixer_body(x_ref, conv0_ref, hre0_ref, him0_ref, gmix_ref, win_ref, convw_ref, a8r_ref, a8i_ref, w1_ref, w2_ref,
                dskip_ref, wglu_ref, ga_ref, gb_ref, wout_ref,
                o_ref, convo_ref, hreo_ref, himo_ref,
                conv_scr, hre_scr, him_scr, *, tseq):
    rows = BATCH_TILE * tseq

    @pl.when(pl.program_id(1) == 0)
    def _():
        conv_scr[...] = conv0_ref[...].reshape((CONV_K - 1) * BATCH_TILE, CONV_CH)
        hre_scr[...] = hre0_ref[...]
        him_scr[...] = him0_ref[...]

    x = pltpu.einshape("btd->tbd", x_ref[...]).reshape(rows, D_MODEL)
    xn = _rmsnorm(x, gmix_ref[...]).astype(BF16)
    proj = _dot(xn, win_ref[...])
    b_gate = proj[:, 0:CONV_CH]
    c_gate = proj[:, CONV_CH:2 * CONV_CH]
    v = proj[:, 2 * CONV_CH:3 * CONV_CH]
    u = proj[:, 3 * CONV_CH:]

    xin = c_gate * v
    xp = jnp.concatenate([conv_scr[...], xin], axis=0)
    w = convw_ref[...]
    conv = w[0:1] * xp[0:rows]
    for k in range(1, CONV_K):
        conv = conv + w[k:k + 1] * xp[k * BATCH_TILE:k * BATCH_TILE + rows]
    conv_scr[...] = xp[rows:]
    y_a = _rmsnorm(b_gate * conv, ga_ref[...]).astype(BF16)

    y_s = _s5_chunked(u, hre_scr, him_scr, a8r_ref, a8i_ref, w1_ref, w2_ref, tseq // CHUNK) + dskip_ref[...] * u
    y_s = jax.nn.gelu(y_s)
    y_s = y_s * jax.nn.sigmoid(_dot(y_s.astype(BF16), wglu_ref[...]))
    y_b = _rmsnorm(y_s, gb_ref[...]).astype(BF16)

    out = x + _dot(jnp.concatenate([y_a, y_b], axis=1), wout_ref[...])
    o_ref[...] = pltpu.einshape("tbd->btd", out.reshape(tseq, BATCH_TILE, D_MODEL))
    convo_ref[...] = conv_scr[...].reshape(CONV_K - 1, BATCH_TILE, CONV_CH)
    hreo_ref[...] = hre_scr[...]
    himo_ref[...] = him_scr[...]


def _mixer(x, conv0, h0_re, h0_im, lw, layer):
    batch, seq, _ = x.shape
    tseq = min(SEQ_TILE, seq)
    assert batch % BATCH_TILE == 0 and seq % tseq == 0 and tseq % CHUNK == 0

    def layer_block(*shape):
        return _resident((None,) + shape, lambda g, s: (layer,) + (0,) * len(shape))

    state_spec = pl.BlockSpec((BATCH_TILE, SSM_LANES), lambda g, s: (g, 0))
    conv_spec = pl.BlockSpec((CONV_K - 1, BATCH_TILE, CONV_CH), lambda g, s: (0, g, 0))
    x_spec = pl.BlockSpec((BATCH_TILE, tseq, D_MODEL), lambda g, s: (g, s, 0))
    return pl.pallas_call(
        functools.partial(_mixer_body, tseq=tseq),
        grid=(batch // BATCH_TILE, seq // tseq),
        in_specs=[
            x_spec, conv_spec, state_spec, state_spec,
            layer_block(1, D_MODEL),
            layer_block(D_MODEL, IN_PROJ_WIDTH),
            layer_block(CONV_K, CONV_CH),
            layer_block(1, SSM_LANES),
            layer_block(1, SSM_LANES),
            layer_block(N_PAIRS, V7X_MXU_DIM, 2 * V7X_MXU_DIM),
            layer_block(N_PAIRS, V7X_MXU_DIM, V7X_MXU_DIM),
            layer_block(1, SSM_WIDTH),
            layer_block(SSM_WIDTH, SSM_WIDTH),
            layer_block(1, CONV_CH),
            layer_block(1, SSM_WIDTH),
            layer_block(D_MODEL, D_MODEL),
        ],
        out_specs=[x_spec, conv_spec, state_spec, state_spec],
        out_shape=[
            jax.ShapeDtypeStruct(x.shape, F32),
            jax.ShapeDtypeStruct(conv0.shape, F32),
            jax.ShapeDtypeStruct(h0_re.shape, F32),
            jax.ShapeDtypeStruct(h0_im.shape, F32),
        ],
        scratch_shapes=[
            pltpu.VMEM(((CONV_K - 1) * BATCH_TILE, CONV_CH), F32),
            pltpu.VMEM((BATCH_TILE, SSM_LANES), F32),
            pltpu.VMEM((BATCH_TILE, SSM_LANES), F32),
        ],
        compiler_params=pltpu.CompilerParams(
            dimension_semantics=("arbitrary", "arbitrary"), vmem_limit_bytes=VMEM_LIMIT),
        name=f"mixer_l{layer}_s{seq}",
    )(x, conv0, h0_re, h0_im, lw["g_mix"], lw["w_in"], lw["conv_w"], lw["a8_re"], lw["a8_im"], lw["w1"], lw["w2"],
      lw["d_skip"], lw["w_glu"], lw["g_grp_a"], lw["g_grp_b"], lw["w_out"])


def _attend_body(x_ref, k_ref, v_ref, g_ref, wq_ref, wo_ref, o_ref, *, bt, tq):
    x = x_ref[...].reshape(bt * tq, D_MODEL)
    xn = _rmsnorm(x, g_ref[...]).astype(BF16)
    q = (_dot(xn, wq_ref[...]) * (MEM_HEAD_DIM ** -0.5)).astype(BF16)
    seqs = []
    for i in range(bt):
        rows = slice(i * tq, (i + 1) * tq)
        k = k_ref[i].astype(BF16)
        v = v_ref[i].astype(BF16)
        heads = []
        for h in range(MEM_HEADS):
            lanes = slice(h * MEM_HEAD_DIM, (h + 1) * MEM_HEAD_DIM)
            s = lax.dot_general(q[rows, lanes], k[:, lanes], (((1,), (1,)), ((), ())), preferred_element_type=F32)
            p = jnp.exp(s - jnp.max(s, axis=-1, keepdims=True))
            o = _dot(p.astype(BF16), v[:, lanes]) / jnp.sum(p, axis=-1, keepdims=True)
            heads.append(o.astype(BF16))
        seqs.append(jnp.concatenate(heads, axis=1))
    out = x + _dot(jnp.concatenate(seqs, axis=0), wo_ref[...])
    o_ref[...] = out.reshape(bt, tq, D_MODEL)


def _attend(x, mem_k, mem_v, lw, layer):
    batch, seq, _ = x.shape
    tq = min(ROW_TILE, seq)
    bt = min(ROW_TILE // tq, batch)
    assert seq % tq == 0 and batch % bt == 0
    kv_spec = pl.BlockSpec((None, bt, N_MEM, D_MODEL), lambda b, s: (layer, b, 0, 0))
    x_spec = pl.BlockSpec((bt, tq, D_MODEL), lambda b, s: (b, s, 0))
    return pl.pallas_call(
        functools.partial(_attend_body, bt=bt, tq=tq),
        grid=(batch // bt, seq // tq),
        in_specs=[
            x_spec,
            kv_spec,
            kv_spec,
            _resident((None, 1, D_MODEL), lambda b, s: (layer, 0, 0)),
            _resident((None, D_MODEL, D_MODEL), lambda b, s: (layer, 0, 0)),
            _resident((None, D_MODEL, D_MODEL), lambda b, s: (layer, 0, 0)),
        ],
        out_specs=x_spec,
        out_shape=jax.ShapeDtypeStruct(x.shape, F32),
        compiler_params=pltpu.CompilerParams(
            dimension_semantics=("arbitrary", "arbitrary"), vmem_limit_bytes=VMEM_LIMIT),
        name=f"attend_l{layer}_s{seq}",
    )(x, mem_k, mem_v, lw["g_xattn"], lw["w_q"], lw["w_o"])


def _mlp_body(x_ref, g_ref, wup_ref, wdown_ref, gfinal_ref, o_ref, *, final_norm):
    x = x_ref[...]
    xn = _rmsnorm(x, g_ref[...]).astype(BF16)
    hdn = jnp.square(jnp.maximum(_dot(xn, wup_ref[...]), 0.0)).astype(BF16)
    out = x + _dot(hdn, wdown_ref[...])
    if final_norm:
        out = _rmsnorm(out, gfinal_ref[...])
    o_ref[...] = out


def _mlp(x, lw, g_final, layer, final_norm):
    shape = x.shape
    x2 = x.reshape(-1, D_MODEL)
    n_rows = x2.shape[0]
    tm = min(ROW_TILE, n_rows)
    assert n_rows % tm == 0
    out = pl.pallas_call(
        functools.partial(_mlp_body, final_norm=final_norm),
        grid=(n_rows // tm,),
        in_specs=[
            pl.BlockSpec((tm, D_MODEL), lambda i: (i, 0)),
            _resident((None, 1, D_MODEL), lambda i: (layer, 0, 0)),
            _resident((None, D_MODEL, D_FF), lambda i: (layer, 0, 0)),
            _resident((None, D_FF, D_MODEL), lambda i: (layer, 0, 0)),
            _resident((1, D_MODEL), lambda i: (0, 0)),
        ],
        out_specs=pl.BlockSpec((tm, D_MODEL), lambda i: (i, 0)),
        out_shape=jax.ShapeDtypeStruct(x2.shape, F32),
        compiler_params=pltpu.CompilerParams(
            dimension_semantics=("arbitrary",), vmem_limit_bytes=VMEM_LIMIT),
        name=f"mlp_l{layer}_r{n_rows}",
    )(x2, lw["g_mlp"], lw["w_up"], lw["w_down"], g_final)
    return out.reshape(shape)


def _memkv_body(mem_ref, g_ref, wk_ref, wv_ref, k_ref, v_ref, kb_ref, vb_ref):
    mn = _rmsnorm(mem_ref[...].reshape(MEMKV_BATCH_TILE * N_MEM, D_MODEL), g_ref[...]).astype(BF16)
    for w_ref, heads_ref, flat_ref in ((wk_ref, k_ref, kb_ref), (wv_ref, v_ref, vb_ref)):
        kv = _dot(mn, w_ref[...])
        flat_ref[...] = kv.reshape(MEMKV_BATCH_TILE, N_MEM, D_MODEL).astype(BF16)
        for h in range(MEM_HEADS):
            heads_ref[:, :, h, :] = kv[:, h * MEM_HEAD_DIM:(h + 1) * MEM_HEAD_DIM].reshape(
                MEMKV_BATCH_TILE, N_MEM, MEM_HEAD_DIM)


def _memkv(mem, g_mem, w_k, w_v):
    batch = mem.shape[0]
    depth = w_k.shape[0]
    assert batch % MEMKV_BATCH_TILE == 0
    heads_shape = jax.ShapeDtypeStruct((depth, batch, N_MEM, MEM_HEADS, MEM_HEAD_DIM), F32)
    heads_spec = pl.BlockSpec((None, MEMKV_BATCH_TILE, N_MEM, MEM_HEADS, MEM_HEAD_DIM), lambda l, b: (l, b, 0, 0, 0))
    flat_shape = jax.ShapeDtypeStruct((depth, batch, N_MEM, D_MODEL), BF16)
    flat_spec = pl.BlockSpec((None, MEMKV_BATCH_TILE, N_MEM, D_MODEL), lambda l, b: (l, b, 0, 0))
    w_spec = pl.BlockSpec((None, D_MODEL, D_MODEL), lambda l, b: (l, 0, 0))
    return pl.pallas_call(
        _memkv_body,
        grid=(depth, batch // MEMKV_BATCH_TILE),
        in_specs=[
            pl.BlockSpec((MEMKV_BATCH_TILE, N_MEM, D_MODEL), lambda l, b: (b, 0, 0)),
            pl.BlockSpec((None, 1, D_MODEL), lambda l, b: (l, 0, 0)),
            w_spec,
            w_spec,
        ],
        out_specs=[heads_spec, heads_spec, flat_spec, flat_spec],
        out_shape=[heads_shape, heads_shape, flat_shape, flat_shape],
        compiler_params=pltpu.CompilerParams(
            dimension_semantics=("arbitrary", "arbitrary"), vmem_limit_bytes=VMEM_LIMIT),
        name="memkv",
    )(mem, g_mem, w_k, w_v)


def _s5_params(a_re, a_im, log_dt, b_re, b_im, c_re, c_im):
    hi = lax.Precision.HIGHEST
    depth = a_re.shape[0]
    eye = jnp.eye(PAIR_GROUPS, dtype=F32)
    lam_re = jnp.minimum(a_re, A_RE_MAX)
    lam_im = a_im
    dt = jnp.exp(log_dt)[..., None]
    k = jnp.arange(CHUNK + 1, dtype=F32).reshape(-1, 1, 1, 1)
    mag = jnp.exp(k * lam_re * dt)
    ang = k * lam_im * dt
    p_re, p_im = mag * jnp.cos(ang), mag * jnp.sin(ang)
    den = lam_re * lam_re + lam_im * lam_im
    f_re = ((p_re[1] - 1.0) * lam_re + p_im[1] * lam_im) / den
    f_im = (p_im[1] * lam_re - (p_re[1] - 1.0) * lam_im) / den
    bb_re = f_re[..., None] * b_re - f_im[..., None] * b_im
    bb_im = f_re[..., None] * b_im + f_im[..., None] * b_re
    q_re, q_im = p_re[:CHUNK, ..., None], p_im[:CHUNK, ..., None]
    e_re = q_re * bb_re - q_im * bb_im
    e_im = q_re * bb_im + q_im * bb_re

    def pairs(m):
        return m.reshape(m.shape[:3] + (N_PAIRS, PAIR_GROUPS) + m.shape[4:])

    to_state = jnp.einsum("jplngsh,gk->lnpghjks", pairs(jnp.stack([e_re[::-1], e_im[::-1]])), eye, precision=hi)
    to_state = to_state.reshape(depth, N_PAIRS, V7X_MXU_DIM, V7X_MXU_DIM)
    kern = (jnp.einsum("lghs,jlgsk->jlgkh", c_re, e_re, precision=hi)
            - jnp.einsum("lghs,jlgsk->jlgkh", c_im, e_im, precision=hi))
    lag = jnp.arange(CHUNK)[None, :] - jnp.arange(CHUNK)[:, None]
    kern_pq = jnp.where((lag >= 0)[:, :, None, None, None, None], kern[jnp.maximum(lag, 0)], 0.0)
    kern_pq = kern_pq.reshape(CHUNK, CHUNK, depth, N_PAIRS, PAIR_GROUPS, SSM_GROUP, SSM_GROUP)
    toeplitz = jnp.einsum("pqlngab,gk->lnpgaqkb", kern_pq, eye, precision=hi).reshape(depth, N_PAIRS, V7X_MXU_DIM, V7X_MXU_DIM)
    w1 = jnp.concatenate([to_state, toeplitz], axis=-1)
    r_re, r_im = p_re[1:, :, :, None, :], p_im[1:, :, :, None, :]
    from_state = jnp.stack([c_re * r_re - c_im * r_im, -(c_re * r_im + c_im * r_re)])
    w2 = jnp.einsum("jqlnghs,gk->lnjgsqkh", pairs(from_state), eye, precision=hi).reshape(
        depth, N_PAIRS, V7X_MXU_DIM, V7X_MXU_DIM)
    return (p_re[CHUNK].reshape(depth, 1, SSM_LANES), p_im[CHUNK].reshape(depth, 1, SSM_LANES),
            w1.astype(BF16), w2.astype(BF16))


def kernel(x_prompt, x_sample, mem_prompt, state_conv, state_ssm_re, state_ssm_im, cache_mem_k, cache_mem_v, g_mix, w_in, conv_w, ssm_a_re, ssm_a_im, ssm_log_dt, ssm_b_re, ssm_b_im, ssm_c_re, ssm_c_im, ssm_d, w_glu, g_grp_a, g_grp_b, w_out, g_xattn, g_mem, w_q, w_k, w_v, w_o, g_mlp, w_up, w_down, g_final):
    depth = w_in.shape[0]
    bp = x_prompt.shape[0]
    bs = x_sample.shape[0]
    a8_re, a8_im, w1, w2 = _s5_params(ssm_a_re, ssm_a_im, ssm_log_dt, ssm_b_re, ssm_b_im, ssm_c_re, ssm_c_im)
    row = lambda g: g.reshape(depth, 1, -1)
    lw = {
        "g_mix": row(g_mix), "w_in": w_in.astype(BF16), "conv_w": conv_w, "a8_re": a8_re, "a8_im": a8_im,
        "w1": w1, "w2": w2, "d_skip": row(ssm_d), "w_glu": w_glu.astype(BF16),
        "g_grp_a": row(g_grp_a), "g_grp_b": row(g_grp_b), "w_out": w_out.astype(BF16),
        "g_xattn": row(g_xattn), "w_q": w_q.astype(BF16), "w_o": w_o.astype(BF16),
        "g_mlp": row(g_mlp), "w_up": w_up.astype(BF16), "w_down": w_down.astype(BF16),
    }
    g_final2 = g_final.reshape(1, D_MODEL)

    mk_out, mv_out, mk_p, mv_p = _memkv(mem_prompt, row(g_mem), w_k.astype(BF16), w_v.astype(BF16))
    mk_s = cache_mem_k.reshape(depth, bs, N_MEM, D_MODEL)
    mv_s = cache_mem_v.reshape(depth, bs, N_MEM, D_MODEL)

    zero_conv = jnp.zeros((CONV_K - 1, bp, CONV_CH), F32)
    zero_h = jnp.zeros((bp, SSM_LANES), F32)
    state_shape = (-1, SSM_GROUPS, SSM_STATE)

    xp, xs = x_prompt, x_sample
    conv_p, re_p, im_p, conv_s, re_s, im_s = [], [], [], [], [], []
    for l in range(depth):
        last = l == depth - 1
        xp, c_new, r_new, i_new = _mixer(xp, zero_conv, zero_h, zero_h, lw, l)
        conv_p.append(jnp.swapaxes(c_new, 0, 1))
        re_p.append(r_new.reshape(state_shape))
        im_p.append(i_new.reshape(state_shape))
        xp = _attend(xp, mk_p, mv_p, lw, l)
        xp = _mlp(xp, lw, g_final2, l, last)

        xs, c_new, r_new, i_new = _mixer(xs, jnp.swapaxes(state_conv[l], 0, 1),
                                         state_ssm_re[l].reshape(bs, SSM_LANES),
                                         state_ssm_im[l].reshape(bs, SSM_LANES), lw, l)
        conv_s.append(jnp.swapaxes(c_new, 0, 1))
        re_s.append(r_new.reshape(state_shape))
        im_s.append(i_new.reshape(state_shape))
        xs = _attend(xs, mk_s, mv_s, lw, l)
        xs = _mlp(xs, lw, g_final2, l, last)

    return (xp, xs,
            jnp.stack(conv_p), jnp.stack(re_p), jnp.stack(im_p), mk_out, mv_out,
            jnp.stack(conv_s), jnp.stack(re_s), jnp.stack(im_s))
```

```python
import functools

import jax
import jax.numpy as jnp
from jax import lax
from jax.experimental import pallas as pl
from jax.experimental.pallas import tpu as pltpu

D_MODEL = 1024
CONV_CH = 512
CONV_K = 3
SSM_WIDTH = 512
SSM_GROUP = 16
SSM_GROUPS = 32
SSM_STATE = 64
N_MEM = 256
MEM_HEADS = 4
MEM_HEAD_DIM = D_MODEL // MEM_HEADS
D_FF = 4 * D_MODEL
IN_PROJ_WIDTH = 3 * CONV_CH + SSM_WIDTH
EPS = 1e-6
A_RE_MAX = -1e-4

F32 = jnp.float32
BF16 = jnp.bfloat16

V7X_F32_SUBLANES = 8
V7X_LANES = 128
V7X_MXU_DIM = 256
V7X_VMEM_BYTES = 64 * 1024 * 1024

PAIR_GROUPS = V7X_MXU_DIM // (2 * SSM_STATE)
PAIR_U = PAIR_GROUPS * SSM_GROUP
PAIR_STATE = PAIR_GROUPS * SSM_STATE
N_PAIRS = SSM_GROUPS // PAIR_GROUPS
CHUNK = V7X_MXU_DIM // PAIR_U
SSM_LANES = SSM_GROUPS * SSM_STATE

LANE_SLABS = D_MODEL // V7X_LANES
BATCH_TILE = V7X_F32_SUBLANES
SEQ_TILE = 128
ROW_TILE = 512
MEMKV_BATCH_TILE = 2
VMEM_LIMIT = V7X_VMEM_BYTES - 8 * 1024 * 1024


def _rmsnorm(x, g):
    inv = lax.rsqrt(jnp.mean(x * x, axis=-1, keepdims=True) + EPS)
    return x * inv * g


def _dot(a, b):
    return jnp.dot(a, b, preferred_element_type=F32)


def _turn_pitch(tseq):
    assert tseq % V7X_F32_SUBLANES == 0
    return tseq + V7X_F32_SUBLANES // 2


def _resident(block_shape, index_map):
    return pl.BlockSpec(block_shape, index_map, pipeline_mode=pl.Buffered(1))


def _s5_chunked(u, hre_scr, him_scr, a8r_ref, a8i_ref, w1_ref, w2_ref, n_chunks):
    crow = n_chunks * BATCH_TILE
    u_pos = u.reshape(n_chunks, CHUNK, BATCH_TILE, SSM_WIDTH)
    u_pos = [u_pos[:, p].reshape(crow, SSM_WIDTH) for p in range(CHUNK)]
    z_re, z_im, y_in = [], [], []
    for gp in range(N_PAIRS):
        lanes = slice(gp * PAIR_U, (gp + 1) * PAIR_U)
        xg = jnp.concatenate([up[:, lanes] for up in u_pos], axis=1).astype(BF16)
        zy = _dot(xg, w1_ref[gp])
        z_re.append(zy[:, 0:PAIR_STATE])
        z_im.append(zy[:, PAIR_STATE:2 * PAIR_STATE])
        y_in.append(zy[:, 2 * PAIR_STATE:])
    z_re = jnp.concatenate(z_re, axis=1)
    z_im = jnp.concatenate(z_im, axis=1)

    a_re = jnp.broadcast_to(a8r_ref[...], (BATCH_TILE, SSM_LANES))
    a_im = jnp.broadcast_to(a8i_ref[...], (BATCH_TILE, SSM_LANES))
    h_re, h_im = hre_scr[...], him_scr[...]
    prev_re, prev_im = [], []
    for c in range(n_chunks):
        prev_re.append(h_re)
        prev_im.append(h_im)
        seqs = slice(c * BATCH_TILE, (c + 1) * BATCH_TILE)
        h_re, h_im = (a_re * h_re - a_im * h_im + z_re[seqs], a_re * h_im + a_im * h_re + z_im[seqs])
    hre_scr[...] = h_re
    him_scr[...] = h_im
    prev_re = jnp.concatenate(prev_re, axis=0)
    prev_im = jnp.concatenate(prev_im, axis=0)

    y_pair = []
    for gp in range(N_PAIRS):
        lanes = slice(gp * PAIR_STATE, (gp + 1) * PAIR_STATE)
        sp = jnp.concatenate([prev_re[:, lanes], prev_im[:, lanes]], axis=1).astype(BF16)
        y_pair.append(y_in[gp] + _dot(sp, w2_ref[gp]))
    y_pos = [jnp.concatenate([yp[:, q * PAIR_U:(q + 1) * PAIR_U] for yp in y_pair], axis=1)
             .reshape(n_chunks, BATCH_TILE, SSM_WIDTH) for q in range(CHUNK)]
    return jnp.stack(y_pos, axis=1).reshape(n_chunks * CHUNK * BATCH_TILE, SSM_WIDTH)


def _mixer_body(x_ref, conv0_ref, hre0_ref, him0_ref, gmix_ref, win_ref, convw_ref, a8r_ref, a8i_ref, w1_ref, w2_ref,
                dskip_ref, wglu_ref, ga_ref, gb_ref, wout_ref,
                o_ref, convo_ref, hreo_ref, himo_ref,
                conv_scr, hre_scr, him_scr, turn_scr, *, tseq):
    rows = BATCH_TILE * tseq

    @pl.when(pl.program_id(1) == 0)
    def _():
        conv_scr[...] = conv0_ref[...].reshape((CONV_K - 1) * BATCH_TILE, CONV_CH)
        hre_scr[...] = hre0_ref[...]
        him_scr[...] = him0_ref[...]

    pitch = _turn_pitch(tseq)
    gmix = gmix_ref[...]
    for b in range(BATCH_TILE):
        xn_b = _rmsnorm(x_ref[b], gmix)
        for j in range(LANE_SLABS):
            turn_scr[j, pl.ds(b * pitch, tseq), :] = xn_b[:, j * V7X_LANES:(j + 1) * V7X_LANES]
    xn = jnp.concatenate(
        [jnp.concatenate([turn_scr[j, pl.ds(t, BATCH_TILE, stride=pitch), :] for t in range(tseq)], axis=0)
         for j in range(LANE_SLABS)], axis=1).astype(BF16)
    proj = _dot(xn, win_ref[...])
    b_gate = proj[:, 0:CONV_CH]
    c_gate = proj[:, CONV_CH:2 * CONV_CH]
    v = proj[:, 2 * CONV_CH:3 * CONV_CH]
    u = proj[:, 3 * CONV_CH:]

    xin = c_gate * v
    xp = jnp.concatenate([conv_scr[...], xin], axis=0)
    w = convw_ref[...]
    conv = w[0:1] * xp[0:rows]
    for k in range(1, CONV_K):
        conv = conv + w[k:k + 1] * xp[k * BATCH_TILE:k * BATCH_TILE + rows]
    conv_scr[...] = xp[rows:]
    y_a = _rmsnorm(b_gate * conv, ga_ref[...]).astype(BF16)

    y_s = _s5_chunked(u, hre_scr, him_scr, a8r_ref, a8i_ref, w1_ref, w2_ref, tseq // CHUNK) + dskip_ref[...] * u
    y_s = jax.nn.gelu(y_s)
    y_s = y_s * jax.nn.sigmoid(_dot(y_s.astype(BF16), wglu_ref[...]))
    y_b = _rmsnorm(y_s, gb_ref[...]).astype(BF16)

    upd = _dot(jnp.concatenate([y_a, y_b], axis=1), wout_ref[...])
    for j in range(LANE_SLABS):
        for t in range(tseq):
            turn_scr[j, pl.ds(t, BATCH_TILE, stride=pitch), :] = upd[t * BATCH_TILE:(t + 1) * BATCH_TILE,
                                                                     j * V7X_LANES:(j + 1) * V7X_LANES]
    for b in range(BATCH_TILE):
        o_ref[b] = x_ref[b] + jnp.concatenate(
            [turn_scr[j, pl.ds(b * pitch, tseq), :] for j in range(LANE_SLABS)], axis=1)
    convo_ref[...] = conv_scr[...].reshape(CONV_K - 1, BATCH_TILE, CONV_CH)
    hreo_ref[...] = hre_scr[...]
    himo_ref[...] = him_scr[...]


def _mixer(x, conv0, h0_re, h0_im, lw, layer):
    batch, seq, _ = x.shape
    tseq = min(SEQ_TILE, seq)
    assert batch % BATCH_TILE == 0 and seq % tseq == 0 and tseq % CHUNK == 0

    def layer_block(*shape):
        return _resident((None,) + shape, lambda g, s: (layer,) + (0,) * len(shape))

    state_spec = pl.BlockSpec((BATCH_TILE, SSM_LANES), lambda g, s: (g, 0))
    conv_spec = pl.BlockSpec((CONV_K - 1, BATCH_TILE, CONV_CH), lambda g, s: (0, g, 0))
    x_spec = pl.BlockSpec((BATCH_TILE, tseq, D_MODEL), lambda g, s: (g, s, 0))
    return pl.pallas_call(
        functools.partial(_mixer_body, tseq=tseq),
        grid=(batch // BATCH_TILE, seq // tseq),
        in_specs=[
            x_spec, conv_spec, state_spec, state_spec,
            layer_block(1, D_MODEL),
            layer_block(D_MODEL, IN_PROJ_WIDTH),
            layer_block(CONV_K, CONV_CH),
            layer_block(1, SSM_LANES),
            layer_block(1, SSM_LANES),
            layer_block(N_PAIRS, V7X_MXU_DIM, 2 * V7X_MXU_DIM),
            layer_block(N_PAIRS, V7X_MXU_DIM, V7X_MXU_DIM),
            layer_block(1, SSM_WIDTH),
            layer_block(SSM_WIDTH, SSM_WIDTH),
            layer_block(1, CONV_CH),
            layer_block(1, SSM_WIDTH),
            layer_block(D_MODEL, D_MODEL),
        ],
        out_specs=[x_spec, conv_spec, state_spec, state_spec],
        out_shape=[
            jax.ShapeDtypeStruct(x.shape, F32),
            jax.ShapeDtypeStruct(conv0.shape, F32),
            jax.ShapeDtypeStruct(h0_re.shape, F32),
            jax.ShapeDtypeStruct(h0_im.shape, F32),
        ],
        scratch_shapes=[
            pltpu.VMEM(((CONV_K - 1) * BATCH_TILE, CONV_CH), F32),
            pltpu.VMEM((BATCH_TILE, SSM_LANES), F32),
            pltpu.VMEM((BATCH_TILE, SSM_LANES), F32),
            pltpu.VMEM((LANE_SLABS, BATCH_TILE * _turn_pitch(tseq), V7X_LANES), F32),
        ],
        compiler_params=pltpu.CompilerParams(
            dimension_semantics=("arbitrary", "arbitrary"), vmem_limit_bytes=VMEM_LIMIT),
        name=f"mixer_l{layer}_s{seq}",
    )(x, conv0, h0_re, h0_im, lw["g_mix"], lw["w_in"], lw["conv_w"], lw["a8_re"], lw["a8_im"], lw["w1"], lw["w2"],
      lw["d_skip"], lw["w_glu"], lw["g_grp_a"], lw["g_grp_b"], lw["w_out"])


def _attend_body(x_ref, k_ref, v_ref, g_ref, wq_ref, wo_ref, o_ref, *, bt, tq):
    x = x_ref[...].reshape(bt * tq, D_MODEL)
    xn = _rmsnorm(x, g_ref[...]).astype(BF16)
    q = (_dot(xn, wq_ref[...]) * (MEM_HEAD_DIM ** -0.5)).astype(BF16)
    seqs = []
    for i in range(bt):
        rows = slice(i * tq, (i + 1) * tq)
        k = k_ref[i].astype(BF16)
        v = v_ref[i].astype(BF16)
        heads = []
        for h in range(MEM_HEADS):
            lanes = slice(h * MEM_HEAD_DIM, (h + 1) * MEM_HEAD_DIM)
            s = lax.dot_general(q[rows, lanes], k[:, lanes], (((1,), (1,)), ((), ())), preferred_element_type=F32)
            p = jnp.exp(s - jnp.max(s, axis=-1, keepdims=True))
            o = _dot(p.astype(BF16), v[:, lanes]) / jnp.sum(p, axis=-1, keepdims=True)
            heads.append(o.astype(BF16))
        seqs.append(jnp.concatenate(heads, axis=1))
    out = x + _dot(jnp.concatenate(seqs, axis=0), wo_ref[...])
    o_ref[...] = out.reshape(bt, tq, D_MODEL)


def _attend(x, mem_k, mem_v, lw, layer):
    batch, seq, _ = x.shape
    tq = min(ROW_TILE, seq)
    bt = min(ROW_TILE // tq, batch)
    assert seq % tq == 0 and batch % bt == 0
    kv_spec = pl.BlockSpec((None, bt, N_MEM, D_MODEL), lambda b, s: (layer, b, 0, 0))
    x_spec = pl.BlockSpec((bt, tq, D_MODEL), lambda b, s: (b, s, 0))
    return pl.pallas_call(
        functools.partial(_attend_body, bt=bt, tq=tq),
        grid=(batch // bt, seq // tq),
        in_specs=[
            x_spec,
            kv_spec,
            kv_spec,
            _resident((None, 1, D_MODEL), lambda b, s: (layer, 0, 0)),
            _resident((None, D_MODEL, D_MODEL), lambda b, s: (layer, 0, 0)),
            _resident((None, D_MODEL, D_MODEL), lambda b, s: (layer, 0, 0)),
        ],
        out_specs=x_spec,
        out_shape=jax.ShapeDtypeStruct(x.shape, F32),
        compiler_params=pltpu.CompilerParams(
            dimension_semantics=("arbitrary", "arbitrary"), vmem_limit_bytes=VMEM_LIMIT),
        name=f"attend_l{layer}_s{seq}",
    )(x, mem_k, mem_v, lw["g_xattn"], lw["w_q"], lw["w_o"])


def _mlp_body(x_ref, g_ref, wup_ref, wdown_ref, gfinal_ref, o_ref, *, final_norm):
    x = x_ref[...]
    xn = _rmsnorm(x, g_ref[...]).astype(BF16)
    hdn = jnp.square(jnp.maximum(_dot(xn, wup_ref[...]), 0.0)).astype(BF16)
    out = x + _dot(hdn, wdown_ref[...])
    if final_norm:
        out = _rmsnorm(out, gfinal_ref[...])
    o_ref[...] = out


def _mlp(x, lw, g_final, layer, final_norm):
    shape = x.shape
    x2 = x.reshape(-1, D_MODEL)
    n_rows = x2.shape[0]
    tm = min(ROW_TILE, n_rows)
    assert n_rows % tm == 0
    out = pl.pallas_call(
        functools.partial(_mlp_body, final_norm=final_norm),
        grid=(n_rows // tm,),
        in_specs=[
            pl.BlockSpec((tm, D_MODEL), lambda i: (i, 0)),
            _resident((None, 1, D_MODEL), lambda i: (layer, 0, 0)),
            _resident((None, D_MODEL, D_FF), lambda i: (layer, 0, 0)),
            _resident((None, D_FF, D_MODEL), lambda i: (layer, 0, 0)),
            _resident((1, D_MODEL), lambda i: (0, 0)),
        ],
        out_specs=pl.BlockSpec((tm, D_MODEL), lambda i: (i, 0)),
        out_shape=jax.ShapeDtypeStruct(x2.shape, F32),
        compiler_params=pltpu.CompilerParams(
            dimension_semantics=("arbitrary",), vmem_limit_bytes=VMEM_LIMIT),
        name=f"mlp_l{layer}_r{n_rows}",
    )(x2, lw["g_mlp"], lw["w_up"], lw["w_down"], g_final)
    return out.reshape(shape)


def _memkv_body(mem_ref, g_ref, wk_ref, wv_ref, k_ref, v_ref, kb_ref, vb_ref):
    mn = _rmsnorm(mem_ref[...].reshape(MEMKV_BATCH_TILE * N_MEM, D_MODEL), g_ref[...]).astype(BF16)
    for w_ref, heads_ref, flat_ref in ((wk_ref, k_ref, kb_ref), (wv_ref, v_ref, vb_ref)):
        kv = _dot(mn, w_ref[...])
        flat_ref[...] = kv.reshape(MEMKV_BATCH_TILE, N_MEM, D_MODEL).astype(BF16)
        for h in range(MEM_HEADS):
            heads_ref[:, :, h, :] = kv[:, h * MEM_HEAD_DIM:(h + 1) * MEM_HEAD_DIM].reshape(
                MEMKV_BATCH_TILE, N_MEM, MEM_HEAD_DIM)


def _memkv(mem, g_mem, w_k, w_v):
    batch = mem.shape[0]
    depth = w_k.shape[0]
    assert batch % MEMKV_BATCH_TILE == 0
    heads_shape = jax.ShapeDtypeStruct((depth, batch, N_MEM, MEM_HEADS, MEM_HEAD_DIM), F32)
    heads_spec = pl.BlockSpec((None, MEMKV_BATCH_TILE, N_MEM, MEM_HEADS, MEM_HEAD_DIM), lambda l, b: (l, b, 0, 0, 0))
    flat_shape = jax.ShapeDtypeStruct((depth, batch, N_MEM, D_MODEL), BF16)
    flat_spec = pl.BlockSpec((None, MEMKV_BATCH_TILE, N_MEM, D_MODEL), lambda l, b: (l, b, 0, 0))
    w_spec = pl.BlockSpec((None, D_MODEL, D_MODEL), lambda l, b: (l, 0, 0))
    return pl.pallas_call(
        _memkv_body,
        grid=(depth, batch // MEMKV_BATCH_TILE),
        in_specs=[
            pl.BlockSpec((MEMKV_BATCH_TILE, N_MEM, D_MODEL), lambda l, b: (b, 0, 0)),
            pl.BlockSpec((None, 1, D_MODEL), lambda l, b: (l, 0, 0)),
            w_spec,
            w_spec,
        ],
        out_specs=[heads_spec, heads_spec, flat_spec, flat_spec],
        out_shape=[heads_shape, heads_shape, flat_shape, flat_shape],
        compiler_params=pltpu.CompilerParams(
            dimension_semantics=("arbitrary", "arbitrary"), vmem_limit_bytes=VMEM_LIMIT),
        name="memkv",
    )(mem, g_mem, w_k, w_v)


def _s5_fold_body(are_ref, aim_ref, logdt_ref, bre_ref, bim_ref, cre_ref, cim_ref,
                  w1_ref, w2_ref, a8re_ref, a8im_ref, w1_scr, w2t_scr):
    w1_scr[...] = jnp.zeros_like(w1_scr)
    w2t_scr[...] = jnp.zeros_like(w2t_scr)
    k = lax.broadcasted_iota(jnp.int32, (2 * CHUNK, SSM_STATE), 0).astype(F32)
    for g in range(PAIR_GROUPS):
        lam_re = jnp.minimum(are_ref[g], A_RE_MAX)
        lam_im = aim_ref[g]
        dt = jnp.exp(logdt_ref[g])
        mag = jnp.exp(k * (lam_re * dt))
        ang = k * (lam_im * dt)
        p_re, p_im = mag * jnp.cos(ang), mag * jnp.sin(ang)
        a8re_ref[:, g * SSM_STATE:(g + 1) * SSM_STATE] = p_re[CHUNK:CHUNK + 1]
        a8im_ref[:, g * SSM_STATE:(g + 1) * SSM_STATE] = p_im[CHUNK:CHUNK + 1]
        den = lam_re * lam_re + lam_im * lam_im
        n_re, n_im = p_re[1:2] - 1.0, p_im[1:2]
        f_re = (n_re * lam_re + n_im * lam_im) / den
        f_im = (n_im * lam_re - n_re * lam_im) / den
        b_re, b_im = bre_ref[g], bim_ref[g]
        bb_re, bb_im = f_re * b_re - f_im * b_im, f_re * b_im + f_im * b_re
        c_re, c_im = cre_ref[g], cim_ref[g]
        e_re, e_im = [], []
        for j in range(CHUNK):
            pj_re, pj_im = p_re[j:j + 1], p_im[j:j + 1]
            e_re.append(pj_re * bb_re - pj_im * bb_im)
            e_im.append(pj_re * bb_im + pj_im * bb_re)
            p = CHUNK - 1 - j
            for part, e in enumerate((e_re[j], e_im[j])):
                w1_scr[pl.ds(p * PAIR_U + g * SSM_GROUP, SSM_GROUP),
                       pl.ds(part * PAIR_STATE + g * SSM_STATE, SSM_STATE)] = e
            r_re, r_im = p_re[j + 1:j + 2], p_im[j + 1:j + 2]
            for part, d in enumerate((c_re * r_re - c_im * r_im, -(c_re * r_im + c_im * r_re))):
                w2t_scr[pl.ds(j * PAIR_U + g * SSM_GROUP, SSM_GROUP),
                        pl.ds(part * PAIR_STATE + g * SSM_STATE, SSM_STATE)] = d
        nt = (((1,), (1,)), ((), ()))
        kern = (lax.dot_general(jnp.concatenate(e_re, axis=0), c_re, nt, precision=lax.Precision.HIGHEST,
                                preferred_element_type=F32)
                - lax.dot_general(jnp.concatenate(e_im, axis=0), c_im, nt, precision=lax.Precision.HIGHEST,
                                  preferred_element_type=F32))
        for j in range(CHUNK):
            k_j = kern[j * SSM_GROUP:(j + 1) * SSM_GROUP]
            for p in range(CHUNK - j):
                w1_scr[pl.ds(p * PAIR_U + g * SSM_GROUP, SSM_GROUP),
                       pl.ds(2 * PAIR_STATE + (p + j) * PAIR_U + g * SSM_GROUP, SSM_GROUP)] = k_j
    w1_ref[...] = w1_scr[...].astype(BF16)
    w2_ref[...] = w2t_scr[...].T.astype(BF16)


def _s5_fold(a_re, a_im, log_dt, b_re, b_im, c_re, c_im):
    depth = a_re.shape[0]
    row = lambda m: m.reshape(depth, SSM_GROUPS, 1, SSM_STATE)
    logdt = jnp.broadcast_to(log_dt[..., None, None], (depth, SSM_GROUPS, 1, SSM_STATE))
    vec_spec = pl.BlockSpec((None, PAIR_GROUPS, 1, SSM_STATE), lambda l, n: (l, n, 0, 0))
    mat_spec = pl.BlockSpec((None, PAIR_GROUPS, SSM_GROUP, SSM_STATE), lambda l, n: (l, n, 0, 0))
    a8_spec = pl.BlockSpec((None, 1, PAIR_STATE), lambda l, n: (l, 0, n))
    a8_shape = jax.ShapeDtypeStruct((depth, 1, SSM_LANES), F32)
    return pl.pallas_call(
        _s5_fold_body,
        grid=(depth, N_PAIRS),
        in_specs=[vec_spec, vec_spec, vec_spec, mat_spec, mat_spec, mat_spec, mat_spec],
        out_specs=[
            pl.BlockSpec((None, None, V7X_MXU_DIM, 2 * V7X_MXU_DIM), lambda l, n: (l, n, 0, 0)),
            pl.BlockSpec((None, None, V7X_MXU_DIM, V7X_MXU_DIM), lambda l, n: (l, n, 0, 0)),
            a8_spec, a8_spec,
        ],
        out_shape=[
            jax.ShapeDtypeStruct((depth, N_PAIRS, V7X_MXU_DIM, 2 * V7X_MXU_DIM), BF16),
            jax.ShapeDtypeStruct((depth, N_PAIRS, V7X_MXU_DIM, V7X_MXU_DIM), BF16),
            a8_shape, a8_shape,
        ],
        scratch_shapes=[
            pltpu.VMEM((V7X_MXU_DIM, 2 * V7X_MXU_DIM), F32),
            pltpu.VMEM((V7X_MXU_DIM, V7X_MXU_DIM), F32),
        ],
        compiler_params=pltpu.CompilerParams(dimension_semantics=("arbitrary", "arbitrary")),
        name="s5_fold",
    )(row(a_re), row(a_im), logdt, jnp.swapaxes(b_re, -1, -2), jnp.swapaxes(b_im, -1, -2), c_re, c_im)


def kernel(x_prompt, x_sample, mem_prompt, state_conv, state_ssm_re, state_ssm_im, cache_mem_k, cache_mem_v, g_mix, w_in, conv_w, ssm_a_re, ssm_a_im, ssm_log_dt, ssm_b_re, ssm_b_im, ssm_c_re, ssm_c_im, ssm_d, w_glu, g_grp_a, g_grp_b, w_out, g_xattn, g_mem, w_q, w_k, w_v, w_o, g_mlp, w_up, w_down, g_final):
    depth = w_in.shape[0]
    bp = x_prompt.shape[0]
    bs = x_sample.shape[0]
    w1, w2, a8_re, a8_im = _s5_fold(ssm_a_re, ssm_a_im, ssm_log_dt, ssm_b_re, ssm_b_im, ssm_c_re, ssm_c_im)
    row = lambda g: g.reshape(depth, 1, -1)
    lw = {
        "g_mix": row(g_mix), "w_in": w_in.astype(BF16), "conv_w": conv_w, "a8_re": a8_re, "a8_im": a8_im,
        "w1": w1, "w2": w2, "d_skip": row(ssm_d), "w_glu": w_glu.astype(BF16),
        "g_grp_a": row(g_grp_a), "g_grp_b": row(g_grp_b), "w_out": w_out.astype(BF16),
        "g_xattn": row(g_xattn), "w_q": w_q.astype(BF16), "w_o": w_o.astype(BF16),
        "g_mlp": row(g_mlp), "w_up": w_up.astype(BF16), "w_down": w_down.astype(BF16),
    }
    g_final2 = g_final.reshape(1, D_MODEL)

    mk_out, mv_out, mk_p, mv_p = _memkv(mem_prompt, row(g_mem), w_k.astype(BF16), w_v.astype(BF16))
    mk_s = cache_mem_k.reshape(depth, bs, N_MEM, D_MODEL)
    mv_s = cache_mem_v.reshape(depth, bs, N_MEM, D_MODEL)

    zero_conv = jnp.zeros((CONV_K - 1, bp, CONV_CH), F32)
    zero_h = jnp.zeros((bp, SSM_LANES), F32)
    state_shape = (-1, SSM_GROUPS, SSM_STATE)

    xp, xs = x_prompt, x_sample
    conv_p, re_p, im_p, conv_s, re_s, im_s = [], [], [], [], [], []
    for l in range(depth):
        last = l == depth - 1
        xp, c_new, r_new, i_new = _mixer(xp, zero_conv, zero_h, zero_h, lw, l)
        conv_p.append(jnp.swapaxes(c_new, 0, 1))
        re_p.append(r_new.reshape(state_shape))
        im_p.append(i_new.reshape(state_shape))
        xp = _attend(xp, mk_p, mv_p, lw, l)
        xp = _mlp(xp, lw, g_final2, l, last)

        xs, c_new, r_new, i_new = _mixer(xs, jnp.swapaxes(state_conv[l], 0, 1),
                                         state_ssm_re[l].reshape(bs, SSM_LANES),
                                         state_ssm_im[l].reshape(bs, SSM_LANES), lw, l)
        conv_s.append(jnp.swapaxes(c_new, 0, 1))
        re_s.append(r_new.reshape(state_shape))
        im_s.append(i_new.reshape(state_shape))
        xs = _attend(xs, mk_s, mv_s, lw, l)
        xs = _mlp(xs, lw, g_final2, l, last)

    return (xp, xs,
            jnp.stack(conv_p), jnp.stack(re_p), jnp.stack(im_p), mk_out, mv_out,
            jnp.stack(conv_s), jnp.stack(re_s), jnp.stack(im_s))
```

```python
import functools

import jax
import jax.numpy as jnp
from jax import lax
from jax.experimental import pallas as pl
from jax.experimental.pallas import tpu as pltpu

D_MODEL = 1024
CONV_CH = 512
CONV_K = 3
SSM_WIDTH = 512
SSM_GROUP = 16
SSM_GROUPS = 32
SSM_STATE = 64
N_MEM = 256
MEM_HEADS = 4
MEM_HEAD_DIM = D_MODEL // MEM_HEADS
D_FF = 4 * D_MODEL
IN_PROJ_WIDTH = 3 * CONV_CH + SSM_WIDTH
EPS = 1e-6
A_RE_MAX = -1e-4

F32 = jnp.float32
BF16 = jnp.bfloat16

V7X_F32_SUBLANES = 8
V7X_LANES = 128
V7X_MXU_DIM = 256
V7X_VMEM_BYTES = 64 * 1024 * 1024

PAIR_GROUPS = V7X_MXU_DIM // (2 * SSM_STATE)
PAIR_U = PAIR_GROUPS * SSM_GROUP
PAIR_STATE = PAIR_GROUPS * SSM_STATE
N_PAIRS = SSM_GROUPS // PAIR_GROUPS
CHUNK = V7X_MXU_DIM // PAIR_U
SSM_LANES = SSM_GROUPS * SSM_STATE

LANE_SLABS = D_MODEL // V7X_LANES
BATCH_TILE = V7X_F32_SUBLANES
SEQ_TILE = 128
ROW_TILE = 512
MEMKV_BATCH_TILE = 2
VMEM_LIMIT = V7X_VMEM_BYTES - 8 * 1024 * 1024


def _rmsnorm(x, g):
    inv = lax.rsqrt(jnp.mean(x * x, axis=-1, keepdims=True) + EPS)
    return x * inv * g


def _dot(a, b):
    return jnp.dot(a, b, preferred_element_type=F32)


def _turn_pitch(tseq):
    assert tseq % V7X_F32_SUBLANES == 0
    return tseq + V7X_F32_SUBLANES // 2


def _resident(block_shape, index_map):
    return pl.BlockSpec(block_shape, index_map, pipeline_mode=pl.Buffered(1))


def _s5_chunked(u, hre_scr, him_scr, a8r_ref, a8i_ref, w1_ref, w2_ref, n_chunks):
    crow = n_chunks * BATCH_TILE
    u_pos = u.reshape(n_chunks, CHUNK, BATCH_TILE, SSM_WIDTH)
    u_pos = [u_pos[:, p].reshape(crow, SSM_WIDTH) for p in range(CHUNK)]
    z_re, z_im, y_in = [], [], []
    for gp in range(N_PAIRS):
        lanes = slice(gp * PAIR_U, (gp + 1) * PAIR_U)
        xg = jnp.concatenate([up[:, lanes] for up in u_pos], axis=1).astype(BF16)
        zy = _dot(xg, w1_ref[gp])
        z_re.append(zy[:, 0:PAIR_STATE])
        z_im.append(zy[:, PAIR_STATE:2 * PAIR_STATE])
        y_in.append(zy[:, 2 * PAIR_STATE:])
    z_re = jnp.concatenate(z_re, axis=1)
    z_im = jnp.concatenate(z_im, axis=1)

    a_re = jnp.broadcast_to(a8r_ref[...], (BATCH_TILE, SSM_LANES))
    a_im = jnp.broadcast_to(a8i_ref[...], (BATCH_TILE, SSM_LANES))
    h_re, h_im = hre_scr[...], him_scr[...]
    prev_re, prev_im = [], []
    for c in range(n_chunks):
        prev_re.append(h_re)
        prev_im.append(h_im)
        seqs = slice(c * BATCH_TILE, (c + 1) * BATCH_TILE)
        h_re, h_im = (a_re * h_re - a_im * h_im + z_re[seqs], a_re * h_im + a_im * h_re + z_im[seqs])
    hre_scr[...] = h_re
    him_scr[...] = h_im
    prev_re = jnp.concatenate(prev_re, axis=0)
    prev_im = jnp.concatenate(prev_im, axis=0)

    y_pair = []
    for gp in range(N_PAIRS):
        lanes = slice(gp * PAIR_STATE, (gp + 1) * PAIR_STATE)
        sp = jnp.concatenate([prev_re[:, lanes], prev_im[:, lanes]], axis=1).astype(BF16)
        y_pair.append((y_in[gp] + _dot(sp, w2_ref[gp])).astype(BF16))
    y_pos = [jnp.concatenate([yp[:, q * PAIR_U:(q + 1) * PAIR_U] for yp in y_pair], axis=1).astype(F32)
             .reshape(n_chunks, BATCH_TILE, SSM_WIDTH) for q in range(CHUNK)]
    return jnp.stack(y_pos, axis=1).reshape(n_chunks * CHUNK * BATCH_TILE, SSM_WIDTH)


def _mixer_body(x_ref, conv0_ref, hre0_ref, him0_ref, gmix_ref, win_ref, convw_ref, a8r_ref, a8i_ref, w1_ref, w2_ref,
                dskip_ref, wglu_ref, ga_ref, gb_ref, wout_ref,
                o_ref, convo_ref, hreo_ref, himo_ref,
                conv_scr, hre_scr, him_scr, turn_scr, *, tseq):
    rows = BATCH_TILE * tseq

    @pl.when(pl.program_id(1) == 0)
    def _():
        conv_scr[...] = conv0_ref[...].reshape((CONV_K - 1) * BATCH_TILE, CONV_CH)
        hre_scr[...] = hre0_ref[...]
        him_scr[...] = him0_ref[...]

    pitch = _turn_pitch(tseq)
    gmix = gmix_ref[...]
    n_parts = 2 if tseq >= 2 * CHUNK * V7X_F32_SUBLANES else 1
    part = tseq // n_parts
    xn, u = [], []
    for i in range(n_parts):
        for b in range(BATCH_TILE):
            xn_b = _rmsnorm(x_ref[b, i * part:(i + 1) * part], gmix)
            for j in range(LANE_SLABS):
                turn_scr[j, pl.ds(b * pitch + i * part, part), :] = xn_b[:, j * V7X_LANES:(j + 1) * V7X_LANES]
        xn.append(jnp.concatenate(
            [jnp.concatenate([turn_scr[j, pl.ds(t, BATCH_TILE, stride=pitch), :]
                              for t in range(i * part, (i + 1) * part)], axis=0)
             for j in range(LANE_SLABS)], axis=1).astype(BF16))
        u.append(_dot(xn[i], win_ref[:, 3 * CONV_CH:]))
    u = jnp.concatenate(u, axis=0)
    cv = jnp.concatenate([_dot(xn_i, win_ref[:, CONV_CH:3 * CONV_CH]) for xn_i in xn], axis=0)
    c_gate = cv[:, 0:CONV_CH]
    v = cv[:, CONV_CH:]
    b_gate = jnp.concatenate([_dot(xn_i, win_ref[:, 0:CONV_CH]) for xn_i in xn], axis=0)

    xin = c_gate * v
    xp = jnp.concatenate([conv_scr[...], xin], axis=0)
    w = convw_ref[...]
    conv = w[0:1] * xp[0:rows]
    for k in range(1, CONV_K):
        conv = conv + w[k:k + 1] * xp[k * BATCH_TILE:k * BATCH_TILE + rows]
    conv_scr[...] = xp[rows:]
    y_a = _rmsnorm(b_gate * conv, ga_ref[...]).astype(BF16)

    y_s = _s5_chunked(u, hre_scr, him_scr, a8r_ref, a8i_ref, w1_ref, w2_ref, tseq // CHUNK) + dskip_ref[...] * u
    y_s = jax.nn.gelu(y_s)
    y_s = y_s * jax.nn.sigmoid(_dot(y_s.astype(BF16), wglu_ref[...]))
    y_b = _rmsnorm(y_s, gb_ref[...]).astype(BF16)

    slabs_per_block = V7X_MXU_DIM // V7X_LANES
    for n in range(D_MODEL // V7X_MXU_DIM):
        cols = slice(n * V7X_MXU_DIM, (n + 1) * V7X_MXU_DIM)
        upd = _dot(y_a, wout_ref[0:CONV_CH, cols]) + _dot(y_b, wout_ref[CONV_CH:, cols])
        for k in range(slabs_per_block):
            j = n * slabs_per_block + k
            for t in range(tseq):
                turn_scr[j, pl.ds(t, BATCH_TILE, stride=pitch), :] = upd[t * BATCH_TILE:(t + 1) * BATCH_TILE,
                                                                         k * V7X_LANES:(k + 1) * V7X_LANES]
        for b in range(BATCH_TILE):
            o_ref[b, :, cols] = x_ref[b, :, cols] + jnp.concatenate(
                [turn_scr[n * slabs_per_block + k, pl.ds(b * pitch, tseq), :] for k in range(slabs_per_block)], axis=1)
    convo_ref[...] = conv_scr[...].reshape(CONV_K - 1, BATCH_TILE, CONV_CH)
    hreo_ref[...] = hre_scr[...]
    himo_ref[...] = him_scr[...]


def _mixer(x, conv0, h0_re, h0_im, lw, layer):
    batch, seq, _ = x.shape
    tseq = min(SEQ_TILE, seq)
    assert batch % BATCH_TILE == 0 and seq % tseq == 0 and tseq % CHUNK == 0

    def layer_block(*shape):
        return _resident((None,) + shape, lambda g, s: (layer,) + (0,) * len(shape))

    state_spec = pl.BlockSpec((BATCH_TILE, SSM_LANES), lambda g, s: (g, 0))
    conv_spec = pl.BlockSpec((CONV_K - 1, BATCH_TILE, CONV_CH), lambda g, s: (0, g, 0))
    x_spec = pl.BlockSpec((BATCH_TILE, tseq, D_MODEL), lambda g, s: (g, s, 0))
    return pl.pallas_call(
        functools.partial(_mixer_body, tseq=tseq),
        grid=(batch // BATCH_TILE, seq // tseq),
        in_specs=[
            x_spec, conv_spec, state_spec, state_spec,
            layer_block(1, D_MODEL),
            layer_block(D_MODEL, IN_PROJ_WIDTH),
            layer_block(CONV_K, CONV_CH),
            layer_block(1, SSM_LANES),
            layer_block(1, SSM_LANES),
            layer_block(N_PAIRS, V7X_MXU_DIM, 2 * V7X_MXU_DIM),
            layer_block(N_PAIRS, V7X_MXU_DIM, V7X_MXU_DIM),
            layer_block(1, SSM_WIDTH),
            layer_block(SSM_WIDTH, SSM_WIDTH),
            layer_block(1, CONV_CH),
            layer_block(1, SSM_WIDTH),
            layer_block(D_MODEL, D_MODEL),
        ],
        out_specs=[x_spec, conv_spec, state_spec, state_spec],
        out_shape=[
            jax.ShapeDtypeStruct(x.shape, F32),
            jax.ShapeDtypeStruct(conv0.shape, F32),
            jax.ShapeDtypeStruct(h0_re.shape, F32),
            jax.ShapeDtypeStruct(h0_im.shape, F32),
        ],
        scratch_shapes=[
            pltpu.VMEM(((CONV_K - 1) * BATCH_TILE, CONV_CH), F32),
            pltpu.VMEM((BATCH_TILE, SSM_LANES), F32),
            pltpu.VMEM((BATCH_TILE, SSM_LANES), F32),
            pltpu.VMEM((LANE_SLABS, BATCH_TILE * _turn_pitch(tseq), V7X_LANES), F32),
        ],
        compiler_params=pltpu.CompilerParams(
            dimension_semantics=("arbitrary", "arbitrary"), vmem_limit_bytes=VMEM_LIMIT),
        name=f"mixer_l{layer}_s{seq}",
    )(x, conv0, h0_re, h0_im, lw["g_mix"], lw["w_in"], lw["conv_w"], lw["a8_re"], lw["a8_im"], lw["w1"], lw["w2"],
      lw["d_skip"], lw["w_glu"], lw["g_grp_a"], lw["g_grp_b"], lw["w_out"])


def _attend_body(x_ref, k_ref, v_ref, g_ref, wq_ref, wo_ref, o_ref, *, bt, tq):
    x = x_ref[...].reshape(bt * tq, D_MODEL)
    xn = _rmsnorm(x, g_ref[...]).astype(BF16)
    q = (_dot(xn, wq_ref[...]) * (MEM_HEAD_DIM ** -0.5)).astype(BF16)
    seqs = []
    for i in range(bt):
        rows = slice(i * tq, (i + 1) * tq)
        k = k_ref[i].astype(BF16)
        v = v_ref[i].astype(BF16)
        heads = []
        for h in range(MEM_HEADS):
            lanes = slice(h * MEM_HEAD_DIM, (h + 1) * MEM_HEAD_DIM)
            s = lax.dot_general(q[rows, lanes], k[:, lanes], (((1,), (1,)), ((), ())), preferred_element_type=F32)
            p = jnp.exp(s - jnp.max(s, axis=-1, keepdims=True))
            o = _dot(p.astype(BF16), v[:, lanes]) / jnp.sum(p, axis=-1, keepdims=True)
            heads.append(o.astype(BF16))
        seqs.append(jnp.concatenate(heads, axis=1))
    out = x + _dot(jnp.concatenate(seqs, axis=0), wo_ref[...])
    o_ref[...] = out.reshape(bt, tq, D_MODEL)


def _attend(x, mem_k, mem_v, lw, layer):
    batch, seq, _ = x.shape
    tq = min(ROW_TILE, seq)
    bt = min(ROW_TILE // tq, batch)
    assert seq % tq == 0 and batch % bt == 0
    kv_spec = pl.BlockSpec((None, bt, N_MEM, D_MODEL), lambda b, s: (layer, b, 0, 0))
    x_spec = pl.BlockSpec((bt, tq, D_MODEL), lambda b, s: (b, s, 0))
    return pl.pallas_call(
        functools.partial(_attend_body, bt=bt, tq=tq),
        grid=(batch // bt, seq // tq),
        in_specs=[
            x_spec,
            kv_spec,
            kv_spec,
            _resident((None, 1, D_MODEL), lambda b, s: (layer, 0, 0)),
            _resident((None, D_MODEL, D_MODEL), lambda b, s: (layer, 0, 0)),
            _resident((None, D_MODEL, D_MODEL), lambda b, s: (layer, 0, 0)),
        ],
        out_specs=x_spec,
        out_shape=jax.ShapeDtypeStruct(x.shape, F32),
        compiler_params=pltpu.CompilerParams(
            dimension_semantics=("arbitrary", "arbitrary"), vmem_limit_bytes=VMEM_LIMIT),
        name=f"attend_l{layer}_s{seq}",
    )(x, mem_k, mem_v, lw["g_xattn"], lw["w_q"], lw["w_o"])


def _mlp_body(x_ref, g_ref, wup_ref, wdown_ref, gfinal_ref, o_ref, *, final_norm):
    x = x_ref[...]
    xn = _rmsnorm(x, g_ref[...]).astype(BF16)
    hdn = jnp.square(jnp.maximum(_dot(xn, wup_ref[...]), 0.0)).astype(BF16)
    out = x + _dot(hdn, wdown_ref[...])
    if final_norm:
        out = _rmsnorm(out, gfinal_ref[...])
    o_ref[...] = out


def _mlp(x, lw, g_final, layer, final_norm):
    shape = x.shape
    x2 = x.reshape(-1, D_MODEL)
    n_rows = x2.shape[0]
    tm = min(ROW_TILE, n_rows)
    assert n_rows % tm == 0
    out = pl.pallas_call(
        functools.partial(_mlp_body, final_norm=final_norm),
        grid=(n_rows // tm,),
        in_specs=[
            pl.BlockSpec((tm, D_MODEL), lambda i: (i, 0)),
            _resident((None, 1, D_MODEL), lambda i: (layer, 0, 0)),
            _resident((None, D_MODEL, D_FF), lambda i: (layer, 0, 0)),
            _resident((None, D_FF, D_MODEL), lambda i: (layer, 0, 0)),
            _resident((1, D_MODEL), lambda i: (0, 0)),
        ],
        out_specs=pl.BlockSpec((tm, D_MODEL), lambda i: (i, 0)),
        out_shape=jax.ShapeDtypeStruct(x2.shape, F32),
        compiler_params=pltpu.CompilerParams(
            dimension_semantics=("arbitrary",), vmem_limit_bytes=VMEM_LIMIT),
        name=f"mlp_l{layer}_r{n_rows}",
    )(x2, lw["g_mlp"], lw["w_up"], lw["w_down"], g_final)
    return out.reshape(shape)


def _memkv_body(mem_ref, g_ref, wk_ref, wv_ref, k_ref, v_ref, kb_ref, vb_ref):
    mn = _rmsnorm(mem_ref[...].reshape(MEMKV_BATCH_TILE * N_MEM, D_MODEL), g_ref[...]).astype(BF16)
    for w_ref, heads_ref, flat_ref in ((wk_ref, k_ref, kb_ref), (wv_ref, v_ref, vb_ref)):
        kv = _dot(mn, w_ref[...])
        flat_ref[...] = kv.reshape(MEMKV_BATCH_TILE, N_MEM, D_MODEL).astype(BF16)
        for h in range(MEM_HEADS):
            heads_ref[:, :, h, :] = kv[:, h * MEM_HEAD_DIM:(h + 1) * MEM_HEAD_DIM].reshape(
                MEMKV_BATCH_TILE, N_MEM, MEM_HEAD_DIM)


def _memkv(mem, g_mem, w_k, w_v):
    batch = mem.shape[0]
    depth = w_k.shape[0]
    assert batch % MEMKV_BATCH_TILE == 0
    heads_shape = jax.ShapeDtypeStruct((depth, batch, N_MEM, MEM_HEADS, MEM_HEAD_DIM), F32)
    heads_spec = pl.BlockSpec((None, MEMKV_BATCH_TILE, N_MEM, MEM_HEADS, MEM_HEAD_DIM), lambda l, b: (l, b, 0, 0, 0))
    flat_shape = jax.ShapeDtypeStruct((depth, batch, N_MEM, D_MODEL), BF16)
    flat_spec = pl.BlockSpec((None, MEMKV_BATCH_TILE, N_MEM, D_MODEL), lambda l, b: (l, b, 0, 0))
    w_spec = pl.BlockSpec((None, D_MODEL, D_MODEL), lambda l, b: (l, 0, 0))
    return pl.pallas_call(
        _memkv_body,
        grid=(depth, batch // MEMKV_BATCH_TILE),
        in_specs=[
            pl.BlockSpec((MEMKV_BATCH_TILE, N_MEM, D_MODEL), lambda l, b: (b, 0, 0)),
            pl.BlockSpec((None, 1, D_MODEL), lambda l, b: (l, 0, 0)),
            w_spec,
            w_spec,
        ],
        out_specs=[heads_spec, heads_spec, flat_spec, flat_spec],
        out_shape=[heads_shape, heads_shape, flat_shape, flat_shape],
        compiler_params=pltpu.CompilerParams(
            dimension_semantics=("arbitrary", "arbitrary"), vmem_limit_bytes=VMEM_LIMIT),
        name="memkv",
    )(mem, g_mem, w_k, w_v)


def _s5_fold_body(are_ref, aim_ref, logdt_ref, bre_ref, bim_ref, cre_ref, cim_ref,
                  w1_ref, w2_ref, a8re_ref, a8im_ref, w1_scr, w2t_scr):
    w1_scr[...] = jnp.zeros_like(w1_scr)
    w2t_scr[...] = jnp.zeros_like(w2t_scr)
    k = lax.broadcasted_iota(jnp.int32, (2 * CHUNK, SSM_STATE), 0).astype(F32)
    for g in range(PAIR_GROUPS):
        lam_re = jnp.minimum(are_ref[g], A_RE_MAX)
        lam_im = aim_ref[g]
        dt = jnp.exp(logdt_ref[g])
        mag = jnp.exp(k * (lam_re * dt))
        ang = k * (lam_im * dt)
        p_re, p_im = mag * jnp.cos(ang), mag * jnp.sin(ang)
        a8re_ref[:, g * SSM_STATE:(g + 1) * SSM_STATE] = p_re[CHUNK:CHUNK + 1]
        a8im_ref[:, g * SSM_STATE:(g + 1) * SSM_STATE] = p_im[CHUNK:CHUNK + 1]
        den = lam_re * lam_re + lam_im * lam_im
        n_re, n_im = p_re[1:2] - 1.0, p_im[1:2]
        f_re = (n_re * lam_re + n_im * lam_im) / den
        f_im = (n_im * lam_re - n_re * lam_im) / den
        b_re, b_im = bre_ref[g], bim_ref[g]
        bb_re, bb_im = f_re * b_re - f_im * b_im, f_re * b_im + f_im * b_re
        c_re, c_im = cre_ref[g], cim_ref[g]
        e_re, e_im = [], []
        for j in range(CHUNK):
            pj_re, pj_im = p_re[j:j + 1], p_im[j:j + 1]
            e_re.append(pj_re * bb_re - pj_im * bb_im)
            e_im.append(pj_re * bb_im + pj_im * bb_re)
            p = CHUNK - 1 - j
            for part, e in enumerate((e_re[j], e_im[j])):
                w1_scr[pl.ds(p * PAIR_U + g * SSM_GROUP, SSM_GROUP),
                       pl.ds(part * PAIR_STATE + g * SSM_STATE, SSM_STATE)] = e
            r_re, r_im = p_re[j + 1:j + 2], p_im[j + 1:j + 2]
            for part, d in enumerate((c_re * r_re - c_im * r_im, -(c_re * r_im + c_im * r_re))):
                w2t_scr[pl.ds(j * PAIR_U + g * SSM_GROUP, SSM_GROUP),
                        pl.ds(part * PAIR_STATE + g * SSM_STATE, SSM_STATE)] = d
        nt = (((1,), (1,)), ((), ()))
        kern = (lax.dot_general(jnp.concatenate(e_re, axis=0), c_re, nt, precision=lax.Precision.HIGHEST,
                                preferred_element_type=F32)
                - lax.dot_general(jnp.concatenate(e_im, axis=0), c_im, nt, precision=lax.Precision.HIGHEST,
                                  preferred_element_type=F32))
        for j in range(CHUNK):
            k_j = kern[j * SSM_GROUP:(j + 1) * SSM_GROUP]
            for p in range(CHUNK - j):
                w1_scr[pl.ds(p * PAIR_U + g * SSM_GROUP, SSM_GROUP),
                       pl.ds(2 * PAIR_STATE + (p + j) * PAIR_U + g * SSM_GROUP, SSM_GROUP)] = k_j
    w1_ref[...] = w1_scr[...].astype(BF16)
    w2_ref[...] = w2t_scr[...].T.astype(BF16)


def _s5_fold(a_re, a_im, log_dt, b_re, b_im, c_re, c_im):
    depth = a_re.shape[0]
    row = lambda m: m.reshape(depth, SSM_GROUPS, 1, SSM_STATE)
    logdt = jnp.broadcast_to(log_dt[..., None, None], (depth, SSM_GROUPS, 1, SSM_STATE))
    vec_spec = pl.BlockSpec((None, PAIR_GROUPS, 1, SSM_STATE), lambda l, n: (l, n, 0, 0))
    mat_spec = pl.BlockSpec((None, PAIR_GROUPS, SSM_GROUP, SSM_STATE), lambda l, n: (l, n, 0, 0))
    a8_spec = pl.BlockSpec((None, 1, PAIR_STATE), lambda l, n: (l, 0, n))
    a8_shape = jax.ShapeDtypeStruct((depth, 1, SSM_LANES), F32)
    return pl.pallas_call(
        _s5_fold_body,
        grid=(depth, N_PAIRS),
        in_specs=[vec_spec, vec_spec, vec_spec, mat_spec, mat_spec, mat_spec, mat_spec],
        out_specs=[
            pl.BlockSpec((None, None, V7X_MXU_DIM, 2 * V7X_MXU_DIM), lambda l, n: (l, n, 0, 0)),
            pl.BlockSpec((None, None, V7X_MXU_DIM, V7X_MXU_DIM), lambda l, n: (l, n, 0, 0)),
            a8_spec, a8_spec,
        ],
        out_shape=[
            jax.ShapeDtypeStruct((depth, N_PAIRS, V7X_MXU_DIM, 2 * V7X_MXU_DIM), BF16),
            jax.ShapeDtypeStruct((depth, N_PAIRS, V7X_MXU_DIM, V7X_MXU_DIM), BF16),
            a8_shape, a8_shape,
        ],
        scratch_shapes=[
            pltpu.VMEM((V7X_MXU_DIM, 2 * V7X_MXU_DIM), F32),
            pltpu.VMEM((V7X_MXU_DIM, V7X_MXU_DIM), F32),
        ],
        compiler_params=pltpu.CompilerParams(dimension_semantics=("arbitrary", "arbitrary")),
        name="s5_fold",
    )(row(a_re), row(a_im), logdt, jnp.swapaxes(b_re, -1, -2), jnp.swapaxes(b_im, -1, -2), c_re, c_im)


def kernel(x_prompt, x_sample, mem_prompt, state_conv, state_ssm_re, state_ssm_im, cache_mem_k, cache_mem_v, g_mix, w_in, conv_w, ssm_a_re, ssm_a_im, ssm_log_dt, ssm_b_re, ssm_b_im, ssm_c_re, ssm_c_im, ssm_d, w_glu, g_grp_a, g_grp_b, w_out, g_xattn, g_mem, w_q, w_k, w_v, w_o, g_mlp, w_up, w_down, g_final):
    depth = w_in.shape[0]
    bp = x_prompt.shape[0]
    bs = x_sample.shape[0]
    w1, w2, a8_re, a8_im = _s5_fold(ssm_a_re, ssm_a_im, ssm_log_dt, ssm_b_re, ssm_b_im, ssm_c_re, ssm_c_im)
    row = lambda g: g.reshape(depth, 1, -1)
    lw = {
        "g_mix": row(g_mix), "w_in": w_in.astype(BF16), "conv_w": conv_w, "a8_re": a8_re, "a8_im": a8_im,
        "w1": w1, "w2": w2, "d_skip": row(ssm_d), "w_glu": w_glu.astype(BF16),
        "g_grp_a": row(g_grp_a), "g_grp_b": row(g_grp_b), "w_out": w_out.astype(BF16),
        "g_xattn": row(g_xattn), "w_q": w_q.astype(BF16), "w_o": w_o.astype(BF16),
        "g_mlp": row(g_mlp), "w_up": w_up.astype(BF16), "w_down": w_down.astype(BF16),
    }
    g_final2 = g_final.reshape(1, D_MODEL)

    mk_out, mv_out, mk_p, mv_p = _memkv(mem_prompt, row(g_mem), w_k.astype(BF16), w_v.astype(BF16))
    mk_s = cache_mem_k.reshape(depth, bs, N_MEM, D_MODEL)
    mv_s = cache_mem_v.reshape(depth, bs, N_MEM, D_MODEL)

    zero_conv = jnp.zeros((CONV_K - 1, bp, CONV_CH), F32)
    zero_h = jnp.zeros((bp, SSM_LANES), F32)
    state_shape = (-1, SSM_GROUPS, SSM_STATE)

    xp, xs = x_prompt, x_sample
    conv_p, re_p, im_p, conv_s, re_s, im_s = [], [], [], [], [], []
    for l in range(depth):
        last = l == depth - 1
        xp, c_new, r_new, i_new = _mixer(xp, zero_conv, zero_h, zero_h, lw, l)
        conv_p.append(jnp.swapaxes(c_new, 0, 1))
        re_p.append(r_new.reshape(state_shape))
        im_p.append(i_new.reshape(state_shape))
        xp = _attend(xp, mk_p, mv_p, lw, l)
        xp = _mlp(xp, lw, g_final2, l, last)

        xs, c_new, r_new, i_new = _mixer(xs, jnp.swapaxes(state_conv[l], 0, 1),
                                         state_ssm_re[l].reshape(bs, SSM_LANES),
                                         state_ssm_im[l].reshape(bs, SSM_LANES), lw, l)
        conv_s.append(jnp.swapaxes(c_new, 0, 1))
        re_s.append(r_new.reshape(state_shape))
        im_s.append(i_new.reshape(state_shape))
        xs = _attend(xs, mk_s, mv_s, lw, l)
        xs = _mlp(xs, lw, g_final2, l, last)

    return (xp, xs,
            jnp.stack(conv_p), jnp.stack(re_p), jnp.stack(im_p), mk_out, mv_out,
            jnp.stack(conv_s), jnp.stack(re_s), jnp.stack(im_s))
```

```python
import functools

import jax
import jax.numpy as jnp
from jax import lax
from jax.experimental import pallas as pl
from jax.experimental.pallas import tpu as pltpu

D_MODEL = 1024
CONV_CH = 512
CONV_K = 3
SSM_WIDTH = 512
SSM_GROUP = 16
SSM_GROUPS = 32
SSM_STATE = 64
N_MEM = 256
MEM_HEADS = 4
MEM_HEAD_DIM = D_MODEL // MEM_HEADS
D_FF = 4 * D_MODEL
IN_PROJ_WIDTH = 3 * CONV_CH + SSM_WIDTH
EPS = 1e-6
A_RE_MAX = -1e-4

F32 = jnp.float32
BF16 = jnp.bfloat16

V7X_F32_SUBLANES = 8
V7X_LANES = 128
V7X_MXU_DIM = 256
V7X_VMEM_BYTES = 64 * 1024 * 1024

PAIR_GROUPS = V7X_MXU_DIM // (2 * SSM_STATE)
PAIR_U = PAIR_GROUPS * SSM_GROUP
PAIR_STATE = PAIR_GROUPS * SSM_STATE
N_PAIRS = SSM_GROUPS // PAIR_GROUPS
CHUNK = V7X_MXU_DIM // PAIR_U
SSM_LANES = SSM_GROUPS * SSM_STATE

LANE_SLABS = D_MODEL // V7X_LANES
BATCH_TILE = V7X_F32_SUBLANES
SEQ_TILE = 128
ROW_TILE = 512
MLP_ROW_TILE = 1024
MLP_HIDDEN_PARTS = 2
ATTEND_SEQ_TILE = 1024
MEMKV_BATCH_TILE = 2
VMEM_LIMIT = V7X_VMEM_BYTES - 8 * 1024 * 1024


def _rmsnorm(x, g):
    inv = lax.rsqrt(jnp.mean(x * x, axis=-1, keepdims=True) + EPS)
    return x * inv * g


def _dot(a, b):
    return jnp.dot(a, b, preferred_element_type=F32)


def _turn_pitch(tseq):
    assert tseq % V7X_F32_SUBLANES == 0
    return tseq + V7X_F32_SUBLANES // 2


def _resident(block_shape, index_map):
    return pl.BlockSpec(block_shape, index_map, pipeline_mode=pl.Buffered(1))


def _s5_states(u, hre_scr, him_scr, a8r_ref, a8i_ref, w1_ref, n_chunks):
    crow = n_chunks * BATCH_TILE
    u_pos = u.reshape(n_chunks, CHUNK, BATCH_TILE, SSM_WIDTH)
    u_pos = [u_pos[:, p].reshape(crow, SSM_WIDTH) for p in range(CHUNK)]
    z_re, z_im, y_in = [], [], []
    for gp in range(N_PAIRS):
        lanes = slice(gp * PAIR_U, (gp + 1) * PAIR_U)
        xg = jnp.concatenate([up[:, lanes] for up in u_pos], axis=1).astype(BF16)
        zy = _dot(xg, w1_ref[gp])
        z_re.append(zy[:, 0:PAIR_STATE])
        z_im.append(zy[:, PAIR_STATE:2 * PAIR_STATE])
        y_in.append(zy[:, 2 * PAIR_STATE:])
    z_re = jnp.concatenate(z_re, axis=1)
    z_im = jnp.concatenate(z_im, axis=1)

    a_re = jnp.broadcast_to(a8r_ref[...], (BATCH_TILE, SSM_LANES))
    a_im = jnp.broadcast_to(a8i_ref[...], (BATCH_TILE, SSM_LANES))
    h_re, h_im = hre_scr[...], him_scr[...]
    prev_re, prev_im = [], []
    for c in range(n_chunks):
        prev_re.append(h_re)
        prev_im.append(h_im)
        seqs = slice(c * BATCH_TILE, (c + 1) * BATCH_TILE)
        h_re, h_im = (a_re * h_re - a_im * h_im + z_re[seqs], a_re * h_im + a_im * h_re + z_im[seqs])
    hre_scr[...] = h_re
    him_scr[...] = h_im
    return y_in, prev_re, prev_im


def _s5_outputs(y_in, prev_re, prev_im, w2_ref, chunks):
    n_chunks = chunks.stop - chunks.start
    rows = slice(chunks.start * BATCH_TILE, chunks.stop * BATCH_TILE)
    prev_re = jnp.concatenate(prev_re[chunks], axis=0)
    prev_im = jnp.concatenate(prev_im[chunks], axis=0)
    y_pair = []
    for gp in range(N_PAIRS):
        lanes = slice(gp * PAIR_STATE, (gp + 1) * PAIR_STATE)
        sp = jnp.concatenate([prev_re[:, lanes], prev_im[:, lanes]], axis=1).astype(BF16)
        y_pair.append((y_in[gp][rows] + _dot(sp, w2_ref[gp])).astype(BF16))
    y_pos = [jnp.concatenate([yp[:, q * PAIR_U:(q + 1) * PAIR_U] for yp in y_pair], axis=1).astype(F32)
             .reshape(n_chunks, BATCH_TILE, SSM_WIDTH) for q in range(CHUNK)]
    return jnp.stack(y_pos, axis=1).reshape(n_chunks * CHUNK * BATCH_TILE, SSM_WIDTH)


def _mixer_body(x_ref, conv0_ref, hre0_ref, him0_ref, gmix_ref, win_ref, convw_ref, a8r_ref, a8i_ref, w1_ref, w2_ref,
                dskip_ref, wglu_ref, ga_ref, gb_ref, wout_ref,
                o_ref, convo_ref, hreo_ref, himo_ref,
                conv_scr, hre_scr, him_scr, turn_scr, *, tseq):
    rows = BATCH_TILE * tseq

    @pl.when(pl.program_id(1) == 0)
    def _():
        conv_scr[...] = conv0_ref[...].reshape((CONV_K - 1) * BATCH_TILE, CONV_CH)
        hre_scr[...] = hre0_ref[...]
        him_scr[...] = him0_ref[...]

    pitch = _turn_pitch(tseq)
    gmix = gmix_ref[...]
    n_parts = 2 if tseq >= 2 * CHUNK * V7X_F32_SUBLANES else 1
    part = tseq // n_parts
    xn, u = [], []
    for i in range(n_parts):
        for b in range(BATCH_TILE):
            xn_b = _rmsnorm(x_ref[b, i * part:(i + 1) * part], gmix)
            for j in range(LANE_SLABS):
                turn_scr[j, pl.ds(b * pitch + i * part, part), :] = xn_b[:, j * V7X_LANES:(j + 1) * V7X_LANES]
        xn.append(jnp.concatenate(
            [jnp.concatenate([turn_scr[j, pl.ds(t, BATCH_TILE, stride=pitch), :]
                              for t in range(i * part, (i + 1) * part)], axis=0)
             for j in range(LANE_SLABS)], axis=1).astype(BF16))
        u.append(_dot(xn[i], win_ref[:, 3 * CONV_CH:]))
    u = jnp.concatenate(u, axis=0)
    cv = jnp.concatenate([_dot(xn_i, win_ref[:, CONV_CH:3 * CONV_CH]) for xn_i in xn], axis=0)
    c_gate = cv[:, 0:CONV_CH]
    v = cv[:, CONV_CH:]
    b_gate = jnp.concatenate([_dot(xn_i, win_ref[:, 0:CONV_CH]) for xn_i in xn], axis=0)

    xin = c_gate * v
    xp = jnp.concatenate([conv_scr[...], xin], axis=0)
    w = convw_ref[...]
    conv = w[0:1] * xp[0:rows]
    for k in range(1, CONV_K):
        conv = conv + w[k:k + 1] * xp[k * BATCH_TILE:k * BATCH_TILE + rows]
    conv_scr[...] = xp[rows:]
    y_a = _rmsnorm(b_gate * conv, ga_ref[...]).astype(BF16)

    n_chunks = tseq // CHUNK
    y_in, prev_re, prev_im = _s5_states(u, hre_scr, him_scr, a8r_ref, a8i_ref, w1_ref, n_chunks)
    y_s = _s5_outputs(y_in, prev_re, prev_im, w2_ref, slice(0, n_chunks)) + dskip_ref[...] * u
    y_s = jax.nn.gelu(y_s)
    y_s = y_s * jax.nn.sigmoid(_dot(y_s.astype(BF16), wglu_ref[...]))
    y_b = _rmsnorm(y_s, gb_ref[...]).astype(BF16)

    slabs_per_block = V7X_MXU_DIM // V7X_LANES
    for n in range(D_MODEL // V7X_MXU_DIM):
        cols = slice(n * V7X_MXU_DIM, (n + 1) * V7X_MXU_DIM)
        upd = _dot(y_a, wout_ref[0:CONV_CH, cols]) + _dot(y_b, wout_ref[CONV_CH:, cols])
        for k in range(slabs_per_block):
            j = n * slabs_per_block + k
            for t in range(tseq):
                turn_scr[j, pl.ds(t, BATCH_TILE, stride=pitch), :] = upd[t * BATCH_TILE:(t + 1) * BATCH_TILE,
                                                                         k * V7X_LANES:(k + 1) * V7X_LANES]
        for b in range(BATCH_TILE):
            o_ref[b, :, cols] = x_ref[b, :, cols] + jnp.concatenate(
                [turn_scr[n * slabs_per_block + k, pl.ds(b * pitch, tseq), :] for k in range(slabs_per_block)], axis=1)
    convo_ref[...] = conv_scr[...].reshape(CONV_K - 1, BATCH_TILE, CONV_CH)
    hreo_ref[...] = hre_scr[...]
    himo_ref[...] = him_scr[...]


def _mixer(x, conv0, h0_re, h0_im, lw, layer):
    batch, seq, _ = x.shape
    tseq = min(SEQ_TILE, seq)
    assert batch % BATCH_TILE == 0 and seq % tseq == 0 and tseq % CHUNK == 0

    def layer_block(*shape):
        return _resident((None,) + shape, lambda g, s: (layer,) + (0,) * len(shape))

    state_spec = pl.BlockSpec((BATCH_TILE, SSM_LANES), lambda g, s: (g, 0))
    conv_spec = pl.BlockSpec((CONV_K - 1, BATCH_TILE, CONV_CH), lambda g, s: (0, g, 0))
    x_spec = pl.BlockSpec((BATCH_TILE, tseq, D_MODEL), lambda g, s: (g, s, 0))
    return pl.pallas_call(
        functools.partial(_mixer_body, tseq=tseq),
        grid=(batch // BATCH_TILE, seq // tseq),
        in_specs=[
            x_spec, conv_spec, state_spec, state_spec,
            layer_block(1, D_MODEL),
            layer_block(D_MODEL, IN_PROJ_WIDTH),
            layer_block(CONV_K, CONV_CH),
            layer_block(1, SSM_LANES),
            layer_block(1, SSM_LANES),
            layer_block(N_PAIRS, V7X_MXU_DIM, 2 * V7X_MXU_DIM),
            layer_block(N_PAIRS, V7X_MXU_DIM, V7X_MXU_DIM),
            layer_block(1, SSM_WIDTH),
            layer_block(SSM_WIDTH, SSM_WIDTH),
            layer_block(1, CONV_CH),
            layer_block(1, SSM_WIDTH),
            layer_block(D_MODEL, D_MODEL),
        ],
        out_specs=[x_spec, conv_spec, state_spec, state_spec],
        out_shape=[
            jax.ShapeDtypeStruct(x.shape, F32),
            jax.ShapeDtypeStruct(conv0.shape, F32),
            jax.ShapeDtypeStruct(h0_re.shape, F32),
            jax.ShapeDtypeStruct(h0_im.shape, F32),
        ],
        scratch_shapes=[
            pltpu.VMEM(((CONV_K - 1) * BATCH_TILE, CONV_CH), F32),
            pltpu.VMEM((BATCH_TILE, SSM_LANES), F32),
            pltpu.VMEM((BATCH_TILE, SSM_LANES), F32),
            pltpu.VMEM((LANE_SLABS, BATCH_TILE * _turn_pitch(tseq), V7X_LANES), F32),
        ],
        compiler_params=pltpu.CompilerParams(
            dimension_semantics=("arbitrary", "arbitrary"), vmem_limit_bytes=VMEM_LIMIT),
        name=f"mixer_l{layer}_s{seq}",
    )(x, conv0, h0_re, h0_im, lw["g_mix"], lw["w_in"], lw["conv_w"], lw["a8_re"], lw["a8_im"], lw["w1"], lw["w2"],
      lw["d_skip"], lw["w_glu"], lw["g_grp_a"], lw["g_grp_b"], lw["w_out"])


def _attend_body(x_ref, k_ref, v_ref, g_ref, wq_ref, wo_ref, o_ref, *, bt, tq):
    x = x_ref[...].reshape(bt * tq, D_MODEL)
    xn = _rmsnorm(x, g_ref[...]).astype(BF16)
    q = (_dot(xn, wq_ref[...]) * (MEM_HEAD_DIM ** -0.5)).astype(BF16)
    seqs = []
    for i in range(bt):
        rows = slice(i * tq, (i + 1) * tq)
        k = k_ref[i].astype(BF16)
        v = v_ref[i].astype(BF16)
        heads = []
        for h in range(MEM_HEADS):
            lanes = slice(h * MEM_HEAD_DIM, (h + 1) * MEM_HEAD_DIM)
            s = lax.dot_general(q[rows, lanes], k[:, lanes], (((1,), (1,)), ((), ())), preferred_element_type=F32)
            p = jnp.exp(s - jnp.max(s, axis=-1, keepdims=True))
            o = _dot(p.astype(BF16), v[:, lanes]) / jnp.sum(p, axis=-1, keepdims=True)
            heads.append(o.astype(BF16))
        seqs.append(jnp.concatenate(heads, axis=1))
    out = x + _dot(jnp.concatenate(seqs, axis=0), wo_ref[...])
    o_ref[...] = out.reshape(bt, tq, D_MODEL)


def _attend(x, mem_k, mem_v, lw, layer):
    batch, seq, _ = x.shape
    tq = min(ATTEND_SEQ_TILE, seq)
    bt = min(max(ROW_TILE // tq, 1), batch)
    assert seq % tq == 0 and batch % bt == 0
    kv_spec = pl.BlockSpec((None, bt, N_MEM, D_MODEL), lambda b, s: (layer, b, 0, 0))
    x_spec = pl.BlockSpec((bt, tq, D_MODEL), lambda b, s: (b, s, 0))
    return pl.pallas_call(
        functools.partial(_attend_body, bt=bt, tq=tq),
        grid=(batch // bt, seq // tq),
        in_specs=[
            x_spec,
            kv_spec,
            kv_spec,
            _resident((None, 1, D_MODEL), lambda b, s: (layer, 0, 0)),
            _resident((None, D_MODEL, D_MODEL), lambda b, s: (layer, 0, 0)),
            _resident((None, D_MODEL, D_MODEL), lambda b, s: (layer, 0, 0)),
        ],
        out_specs=x_spec,
        out_shape=jax.ShapeDtypeStruct(x.shape, F32),
        compiler_params=pltpu.CompilerParams(
            dimension_semantics=("arbitrary", "arbitrary"), vmem_limit_bytes=VMEM_LIMIT),
        name=f"attend_l{layer}_s{seq}",
    )(x, mem_k, mem_v, lw["g_xattn"], lw["w_q"], lw["w_o"])


def _mlp_body(x_ref, g_ref, wup_ref, wdown_ref, gfinal_ref, o_ref, *, final_norm):
    x = x_ref[...]
    xn = _rmsnorm(x, g_ref[...]).astype(BF16)
    out = x
    hid = D_FF // MLP_HIDDEN_PARTS
    for c in range(MLP_HIDDEN_PARTS):
        hdn = jnp.square(jnp.maximum(_dot(xn, wup_ref[:, c * hid:(c + 1) * hid]), 0.0)).astype(BF16)
        out = out + _dot(hdn, wdown_ref[c * hid:(c + 1) * hid, :])
    if final_norm:
        out = _rmsnorm(out, gfinal_ref[...])
    o_ref[...] = out


def _mlp(x, lw, g_final, layer, final_norm):
    shape = x.shape
    x2 = x.reshape(-1, D_MODEL)
    n_rows = x2.shape[0]
    tm = min(MLP_ROW_TILE, n_rows)
    assert n_rows % tm == 0
    out = pl.pallas_call(
        functools.partial(_mlp_body, final_norm=final_norm),
        grid=(n_rows // tm,),
        in_specs=[
            pl.BlockSpec((tm, D_MODEL), lambda i: (i, 0)),
            _resident((None, 1, D_MODEL), lambda i: (layer, 0, 0)),
            _resident((None, D_MODEL, D_FF), lambda i: (layer, 0, 0)),
            _resident((None, D_FF, D_MODEL), lambda i: (layer, 0, 0)),
            _resident((1, D_MODEL), lambda i: (0, 0)),
        ],
        out_specs=pl.BlockSpec((tm, D_MODEL), lambda i: (i, 0)),
        out_shape=jax.ShapeDtypeStruct(x2.shape, F32),
        compiler_params=pltpu.CompilerParams(
            dimension_semantics=("arbitrary",), vmem_limit_bytes=VMEM_LIMIT),
        name=f"mlp_l{layer}_r{n_rows}",
    )(x2, lw["g_mlp"], lw["w_up"], lw["w_down"], g_final)
    return out.reshape(shape)


def _memkv_body(mem_ref, g_ref, wk_ref, wv_ref, k_ref, v_ref, kb_ref, vb_ref):
    mn = _rmsnorm(mem_ref[...].reshape(MEMKV_BATCH_TILE * N_MEM, D_MODEL), g_ref[...]).astype(BF16)
    for w_ref, heads_ref, flat_ref in ((wk_ref, k_ref, kb_ref), (wv_ref, v_ref, vb_ref)):
        kv = _dot(mn, w_ref[...])
        flat_ref[...] = kv.reshape(MEMKV_BATCH_TILE, N_MEM, D_MODEL).astype(BF16)
        for h in range(MEM_HEADS):
            heads_ref[:, :, h, :] = kv[:, h * MEM_HEAD_DIM:(h + 1) * MEM_HEAD_DIM].reshape(
                MEMKV_BATCH_TILE, N_MEM, MEM_HEAD_DIM)


def _memkv(mem, g_mem, w_k, w_v):
    batch = mem.shape[0]
    depth = w_k.shape[0]
    assert batch % MEMKV_BATCH_TILE == 0
    heads_shape = jax.ShapeDtypeStruct((depth, batch, N_MEM, MEM_HEADS, MEM_HEAD_DIM), F32)
    heads_spec = pl.BlockSpec((None, MEMKV_BATCH_TILE, N_MEM, MEM_HEADS, MEM_HEAD_DIM), lambda l, b: (l, b, 0, 0, 0))
    flat_shape = jax.ShapeDtypeStruct((depth, batch, N_MEM, D_MODEL), BF16)
    flat_spec = pl.BlockSpec((None, MEMKV_BATCH_TILE, N_MEM, D_MODEL), lambda l, b: (l, b, 0, 0))
    w_spec = pl.BlockSpec((None, D_MODEL, D_MODEL), lambda l, b: (l, 0, 0))
    return pl.pallas_call(
        _memkv_body,
        grid=(depth, batch // MEMKV_BATCH_TILE),
        in_specs=[
            pl.BlockSpec((MEMKV_BATCH_TILE, N_MEM, D_MODEL), lambda l, b: (b, 0, 0)),
            pl.BlockSpec((None, 1, D_MODEL), lambda l, b: (l, 0, 0)),
            w_spec,
            w_spec,
        ],
        out_specs=[heads_spec, heads_spec, flat_spec, flat_spec],
        out_shape=[heads_shape, heads_shape, flat_shape, flat_shape],
        compiler_params=pltpu.CompilerParams(
            dimension_semantics=("arbitrary", "arbitrary"), vmem_limit_bytes=VMEM_LIMIT),
        name="memkv",
    )(mem, g_mem, w_k, w_v)


def _s5_fold_body(are_ref, aim_ref, logdt_ref, bre_ref, bim_ref, cre_ref, cim_ref,
                  w1_ref, w2_ref, a8re_ref, a8im_ref, w1_scr, w2t_scr):
    w1_scr[...] = jnp.zeros_like(w1_scr)
    w2t_scr[...] = jnp.zeros_like(w2t_scr)
    k = lax.broadcasted_iota(jnp.int32, (2 * CHUNK, SSM_STATE), 0).astype(F32)
    for g in range(PAIR_GROUPS):
        lam_re = jnp.minimum(are_ref[g], A_RE_MAX)
        lam_im = aim_ref[g]
        dt = jnp.exp(logdt_ref[g])
        mag = jnp.exp(k * (lam_re * dt))
        ang = k * (lam_im * dt)
        p_re, p_im = mag * jnp.cos(ang), mag * jnp.sin(ang)
        a8re_ref[:, g * SSM_STATE:(g + 1) * SSM_STATE] = p_re[CHUNK:CHUNK + 1]
        a8im_ref[:, g * SSM_STATE:(g + 1) * SSM_STATE] = p_im[CHUNK:CHUNK + 1]
        den = lam_re * lam_re + lam_im * lam_im
        n_re, n_im = p_re[1:2] - 1.0, p_im[1:2]
        f_re = (n_re * lam_re + n_im * lam_im) / den
        f_im = (n_im * lam_re - n_re * lam_im) / den
        b_re, b_im = bre_ref[g], bim_ref[g]
        bb_re, bb_im = f_re * b_re - f_im * b_im, f_re * b_im + f_im * b_re
        c_re, c_im = cre_ref[g], cim_ref[g]
        e_re, e_im = [], []
        for j in range(CHUNK):
            pj_re, pj_im = p_re[j:j + 1], p_im[j:j + 1]
            e_re.append(pj_re * bb_re - pj_im * bb_im)
            e_im.append(pj_re * bb_im + pj_im * bb_re)
            p = CHUNK - 1 - j
            for part, e in enumerate((e_re[j], e_im[j])):
                w1_scr[pl.ds(p * PAIR_U + g * SSM_GROUP, SSM_GROUP),
                       pl.ds(part * PAIR_STATE + g * SSM_STATE, SSM_STATE)] = e
            r_re, r_im = p_re[j + 1:j + 2], p_im[j + 1:j + 2]
            for part, d in enumerate((c_re * r_re - c_im * r_im, -(c_re * r_im + c_im * r_re))):
                w2t_scr[pl.ds(j * PAIR_U + g * SSM_GROUP, SSM_GROUP),
                        pl.ds(part * PAIR_STATE + g * SSM_STATE, SSM_STATE)] = d
        nt = (((1,), (1,)), ((), ()))
        kern = (lax.dot_general(jnp.concatenate(e_re, axis=0), c_re, nt, precision=lax.Precision.HIGHEST,
                                preferred_element_type=F32)
                - lax.dot_general(jnp.concatenate(e_im, axis=0), c_im, nt, precision=lax.Precision.HIGHEST,
                                  preferred_element_type=F32))
        for j in range(CHUNK):
            k_j = kern[j * SSM_GROUP:(j + 1) * SSM_GROUP]
            for p in range(CHUNK - j):
                w1_scr[pl.ds(p * PAIR_U + g * SSM_GROUP, SSM_GROUP),
                       pl.ds(2 * PAIR_STATE + (p + j) * PAIR_U + g * SSM_GROUP, SSM_GROUP)] = k_j
    w1_ref[...] = w1_scr[...].astype(BF16)
    w2_ref[...] = w2t_scr[...].T.astype(BF16)


def _s5_fold(a_re, a_im, log_dt, b_re, b_im, c_re, c_im):
    depth = a_re.shape[0]
    row = lambda m: m.reshape(depth, SSM_GROUPS, 1, SSM_STATE)
    logdt = jnp.broadcast_to(log_dt[..., None, None], (depth, SSM_GROUPS, 1, SSM_STATE))
    vec_spec = pl.BlockSpec((None, PAIR_GROUPS, 1, SSM_STATE), lambda l, n: (l, n, 0, 0))
    mat_spec = pl.BlockSpec((None, PAIR_GROUPS, SSM_GROUP, SSM_STATE), lambda l, n: (l, n, 0, 0))
    a8_spec = pl.BlockSpec((None, 1, PAIR_STATE), lambda l, n: (l, 0, n))
    a8_shape = jax.ShapeDtypeStruct((depth, 1, SSM_LANES), F32)
    return pl.pallas_call(
        _s5_fold_body,
        grid=(depth, N_PAIRS),
        in_specs=[vec_spec, vec_spec, vec_spec, mat_spec, mat_spec, mat_spec, mat_spec],
        out_specs=[
            pl.BlockSpec((None, None, V7X_MXU_DIM, 2 * V7X_MXU_DIM), lambda l, n: (l, n, 0, 0)),
            pl.BlockSpec((None, None, V7X_MXU_DIM, V7X_MXU_DIM), lambda l, n: (l, n, 0, 0)),
            a8_spec, a8_spec,
        ],
        out_shape=[
            jax.ShapeDtypeStruct((depth, N_PAIRS, V7X_MXU_DIM, 2 * V7X_MXU_DIM), BF16),
            jax.ShapeDtypeStruct((depth, N_PAIRS, V7X_MXU_DIM, V7X_MXU_DIM), BF16),
            a8_shape, a8_shape,
        ],
        scratch_shapes=[
            pltpu.VMEM((V7X_MXU_DIM, 2 * V7X_MXU_DIM), F32),
            pltpu.VMEM((V7X_MXU_DIM, V7X_MXU_DIM), F32),
        ],
        compiler_params=pltpu.CompilerParams(dimension_semantics=("arbitrary", "arbitrary")),
        name="s5_fold",
    )(row(a_re), row(a_im), logdt, jnp.swapaxes(b_re, -1, -2), jnp.swapaxes(b_im, -1, -2), c_re, c_im)


def kernel(x_prompt, x_sample, mem_prompt, state_conv, state_ssm_re, state_ssm_im, cache_mem_k, cache_mem_v, g_mix, w_in, conv_w, ssm_a_re, ssm_a_im, ssm_log_dt, ssm_b_re, ssm_b_im, ssm_c_re, ssm_c_im, ssm_d, w_glu, g_grp_a, g_grp_b, w_out, g_xattn, g_mem, w_q, w_k, w_v, w_o, g_mlp, w_up, w_down, g_final):
    depth = w_in.shape[0]
    bp = x_prompt.shape[0]
    bs = x_sample.shape[0]
    w1, w2, a8_re, a8_im = _s5_fold(ssm_a_re, ssm_a_im, ssm_log_dt, ssm_b_re, ssm_b_im, ssm_c_re, ssm_c_im)
    row = lambda g: g.reshape(depth, 1, -1)
    lw = {
        "g_mix": row(g_mix), "w_in": w_in.astype(BF16), "conv_w": conv_w, "a8_re": a8_re, "a8_im": a8_im,
        "w1": w1, "w2": w2, "d_skip": row(ssm_d), "w_glu": w_glu.astype(BF16),
        "g_grp_a": row(g_grp_a), "g_grp_b": row(g_grp_b), "w_out": w_out.astype(BF16),
        "g_xattn": row(g_xattn), "w_q": w_q.astype(BF16), "w_o": w_o.astype(BF16),
        "g_mlp": row(g_mlp), "w_up": w_up.astype(BF16), "w_down": w_down.astype(BF16),
    }
    g_final2 = g_final.reshape(1, D_MODEL)

    mk_out, mv_out, mk_p, mv_p = _memkv(mem_prompt, row(g_mem), w_k.astype(BF16), w_v.astype(BF16))
    mk_s = cache_mem_k.reshape(depth, bs, N_MEM, D_MODEL)
    mv_s = cache_mem_v.reshape(depth, bs, N_MEM, D_MODEL)

    zero_conv = jnp.zeros((CONV_K - 1, bp, CONV_CH), F32)
    zero_h = jnp.zeros((bp, SSM_LANES), F32)
    state_shape = (-1, SSM_GROUPS, SSM_STATE)

    xp, xs = x_prompt, x_sample
    conv_p, re_p, im_p, conv_s, re_s, im_s = [], [], [], [], [], []
    for l in range(depth):
        last = l == depth - 1
        xp, c_new, r_new, i_new = _mixer(xp, zero_conv, zero_h, zero_h, lw, l)
        conv_p.append(jnp.swapaxes(c_new, 0, 1))
        re_p.append(r_new.reshape(state_shape))
        im_p.append(i_new.reshape(state_shape))
        xp = _attend(xp, mk_p, mv_p, lw, l)
        xp = _mlp(xp, lw, g_final2, l, last)

        xs, c_new, r_new, i_new = _mixer(xs, jnp.swapaxes(state_conv[l], 0, 1),
                                         state_ssm_re[l].reshape(bs, SSM_LANES),
                                         state_ssm_im[l].reshape(bs, SSM_LANES), lw, l)
        conv_s.append(jnp.swapaxes(c_new, 0, 1))
        re_s.append(r_new.reshape(state_shape))
        im_s.append(i_new.reshape(state_shape))
        xs = _attend(xs, mk_s, mv_s, lw, l)
        xs = _mlp(xs, lw, g_final2, l, last)

    return (xp, xs,
            jnp.stack(conv_p), jnp.stack(re_p), jnp.stack(im_p), mk_out, mv_out,
            jnp.stack(conv_s), jnp.stack(re_s), jnp.stack(im_s))
```

```python
import functools

import jax
import jax.numpy as jnp
from jax import lax
from jax.experimental import pallas as pl
from jax.experimental.pallas import tpu as pltpu

D_MODEL = 1024
CONV_CH = 512
CONV_K = 3
SSM_WIDTH = 512
SSM_GROUP = 16
SSM_GROUPS = 32
SSM_STATE = 64
N_MEM = 256
MEM_HEADS = 4
MEM_HEAD_DIM = D_MODEL // MEM_HEADS
D_FF = 4 * D_MODEL
IN_PROJ_WIDTH = 3 * CONV_CH + SSM_WIDTH
EPS = 1e-6
A_RE_MAX = -1e-4

F32 = jnp.float32
BF16 = jnp.bfloat16

V7X_F32_SUBLANES = 8
V7X_LANES = 128
V7X_MXU_DIM = 256
V7X_VMEM_BYTES = 64 * 1024 * 1024

PAIR_GROUPS = V7X_MXU_DIM // (2 * SSM_STATE)
PAIR_U = PAIR_GROUPS * SSM_GROUP
PAIR_STATE = PAIR_GROUPS * SSM_STATE
N_PAIRS = SSM_GROUPS // PAIR_GROUPS
CHUNK = V7X_MXU_DIM // PAIR_U
SSM_LANES = SSM_GROUPS * SSM_STATE

LANE_SLABS = D_MODEL // V7X_LANES
BATCH_TILE = V7X_F32_SUBLANES
SEQ_TILE = 128
HEAD_PARTS = 2
ROW_TILE = 512
MLP_ROW_TILE = 1024
MLP_HIDDEN_PARTS = 2
ATTEND_SEQ_TILE = 1024
MEMKV_BATCH_TILE = 2
VMEM_LIMIT = V7X_VMEM_BYTES - 8 * 1024 * 1024


def _rmsnorm(x, g):
    inv = lax.rsqrt(jnp.mean(x * x, axis=-1, keepdims=True) + EPS)
    return x * inv * g


def _dot(a, b):
    return jnp.dot(a, b, preferred_element_type=F32)


def _turn_pitch(tseq):
    assert tseq % V7X_F32_SUBLANES == 0
    return tseq + V7X_F32_SUBLANES // 2


def _resident(block_shape, index_map):
    return pl.BlockSpec(block_shape, index_map, pipeline_mode=pl.Buffered(1))


def _s5_regroup_in(u, n_chunks):
    crow = n_chunks * BATCH_TILE
    u_pos = u.reshape(n_chunks, CHUNK, BATCH_TILE, SSM_WIDTH)
    u_pos = [u_pos[:, p].reshape(crow, SSM_WIDTH) for p in range(CHUNK)]
    return [jnp.concatenate([up[:, gp * PAIR_U:(gp + 1) * PAIR_U] for up in u_pos], axis=1).astype(BF16)
            for gp in range(N_PAIRS)]


def _s5_chunk_inputs(xg, w1_ref):
    zy = [_dot(xg[gp], w1_ref[gp]) for gp in range(N_PAIRS)]
    z_re = jnp.concatenate([m[:, 0:PAIR_STATE] for m in zy], axis=1)
    z_im = jnp.concatenate([m[:, PAIR_STATE:2 * PAIR_STATE] for m in zy], axis=1)
    return z_re, z_im, [m[:, 2 * PAIR_STATE:] for m in zy]


def _s5_scan(z_re, z_im, hre_scr, him_scr, a8r_ref, a8i_ref, n_chunks):
    a_re = jnp.broadcast_to(a8r_ref[...], (BATCH_TILE, SSM_LANES))
    a_im = jnp.broadcast_to(a8i_ref[...], (BATCH_TILE, SSM_LANES))
    h_re, h_im = hre_scr[...], him_scr[...]
    prev_re, prev_im = [], []
    for c in range(n_chunks):
        prev_re.append(h_re)
        prev_im.append(h_im)
        seqs = slice(c * BATCH_TILE, (c + 1) * BATCH_TILE)
        h_re, h_im = (a_re * h_re - a_im * h_im + z_re[seqs], a_re * h_im + a_im * h_re + z_im[seqs])
    hre_scr[...] = h_re
    him_scr[...] = h_im
    return jnp.concatenate(prev_re, axis=0), jnp.concatenate(prev_im, axis=0)


def _s5_chunk_outputs(y_in, prev_re, prev_im, w2_ref):
    y_pair = []
    for gp in range(N_PAIRS):
        lanes = slice(gp * PAIR_STATE, (gp + 1) * PAIR_STATE)
        sp = jnp.concatenate([prev_re[:, lanes], prev_im[:, lanes]], axis=1).astype(BF16)
        y_pair.append((y_in[gp] + _dot(sp, w2_ref[gp])).astype(BF16))
    return y_pair


def _s5_regroup_out(y_pair, n_chunks):
    y_pos = [jnp.concatenate([yp[:, q * PAIR_U:(q + 1) * PAIR_U] for yp in y_pair], axis=1).astype(F32)
             .reshape(n_chunks, BATCH_TILE, SSM_WIDTH) for q in range(CHUNK)]
    return jnp.stack(y_pos, axis=1).reshape(n_chunks * CHUNK * BATCH_TILE, SSM_WIDTH)


def _mixer_body(x_ref, conv0_ref, hre0_ref, him0_ref, gmix_ref, win_ref, convw_ref, a8r_ref, a8i_ref, w1_ref, w2_ref,
                dskip_ref, wglu_ref, ga_ref, gb_ref, wout_ref,
                o_ref, convo_ref, hreo_ref, himo_ref,
                conv_scr, hre_scr, him_scr, turn_scr, *, tseq):
    rows = BATCH_TILE * tseq

    @pl.when(pl.program_id(1) == 0)
    def _():
        conv_scr[...] = conv0_ref[...].reshape((CONV_K - 1) * BATCH_TILE, CONV_CH)
        hre_scr[...] = hre0_ref[...]
        him_scr[...] = him0_ref[...]

    pitch = _turn_pitch(tseq)
    gmix = gmix_ref[...]
    n_parts = HEAD_PARTS if tseq == SEQ_TILE else 1
    part = tseq // n_parts
    xn, u = [], []
    for i in range(n_parts):
        for b in range(BATCH_TILE):
            xn_b = _rmsnorm(x_ref[b, i * part:(i + 1) * part], gmix)
            for j in range(LANE_SLABS):
                turn_scr[j, pl.ds(b * pitch + i * part, part), :] = xn_b[:, j * V7X_LANES:(j + 1) * V7X_LANES]
        xn.append(jnp.concatenate(
            [jnp.concatenate([turn_scr[j, pl.ds(t, BATCH_TILE, stride=pitch), :]
                              for t in range(i * part, (i + 1) * part)], axis=0)
             for j in range(LANE_SLABS)], axis=1).astype(BF16))
        u.append(_dot(xn[i], win_ref[:, 3 * CONV_CH:]))
    u = jnp.concatenate(u, axis=0)

    def in_proj(block):
        cols = slice(block * CONV_CH, (block + 1) * CONV_CH)
        return jnp.concatenate([_dot(xn_i, win_ref[:, cols]) for xn_i in xn], axis=0)

    n_chunks = tseq // CHUNK
    xg = _s5_regroup_in(u, n_chunks)
    c_gate = in_proj(1)
    z_re, z_im, y_in = _s5_chunk_inputs(xg, w1_ref)
    v = in_proj(2)
    prev_re, prev_im = _s5_scan(z_re, z_im, hre_scr, him_scr, a8r_ref, a8i_ref, n_chunks)

    xin = c_gate * v
    xp = jnp.concatenate([conv_scr[...], xin], axis=0)
    w = convw_ref[...]
    conv = w[0:1] * xp[0:rows]
    for k in range(1, CONV_K):
        conv = conv + w[k:k + 1] * xp[k * BATCH_TILE:k * BATCH_TILE + rows]
    conv_scr[...] = xp[rows:]

    y_pair = _s5_chunk_outputs(y_in, prev_re, prev_im, w2_ref)
    b_gate = in_proj(0)
    y_s = _s5_regroup_out(y_pair, n_chunks) + dskip_ref[...] * u
    y_a = _rmsnorm(b_gate * conv, ga_ref[...]).astype(BF16)
    y_s = jax.nn.gelu(y_s)
    gate = _dot(y_s.astype(BF16), wglu_ref[...])
    upd_a = _dot(y_a, wout_ref[0:CONV_CH, :])
    y_b = _rmsnorm(y_s * jax.nn.sigmoid(gate), gb_ref[...]).astype(BF16)

    slabs_per_block = V7X_MXU_DIM // V7X_LANES
    for n in range(D_MODEL // V7X_MXU_DIM):
        cols = slice(n * V7X_MXU_DIM, (n + 1) * V7X_MXU_DIM)
        upd = upd_a[:, cols] + _dot(y_b, wout_ref[CONV_CH:, cols])
        for k in range(slabs_per_block):
            j = n * slabs_per_block + k
            for t in range(tseq):
                turn_scr[j, pl.ds(t, BATCH_TILE, stride=pitch), :] = upd[t * BATCH_TILE:(t + 1) * BATCH_TILE,
                                                                         k * V7X_LANES:(k + 1) * V7X_LANES]
        for b in range(BATCH_TILE):
            o_ref[b, :, cols] = x_ref[b, :, cols] + jnp.concatenate(
                [turn_scr[n * slabs_per_block + k, pl.ds(b * pitch, tseq), :] for k in range(slabs_per_block)], axis=1)
    convo_ref[...] = conv_scr[...].reshape(CONV_K - 1, BATCH_TILE, CONV_CH)
    hreo_ref[...] = hre_scr[...]
    himo_ref[...] = him_scr[...]


def _mixer(x, conv0, h0_re, h0_im, lw, layer):
    batch, seq, _ = x.shape
    tseq = min(SEQ_TILE, seq)
    assert batch % BATCH_TILE == 0 and seq % tseq == 0 and tseq % CHUNK == 0

    def layer_block(*shape):
        return _resident((None,) + shape, lambda g, s: (layer,) + (0,) * len(shape))

    state_spec = pl.BlockSpec((BATCH_TILE, SSM_LANES), lambda g, s: (g, 0))
    conv_spec = pl.BlockSpec((CONV_K - 1, BATCH_TILE, CONV_CH), lambda g, s: (0, g, 0))
    x_spec = pl.BlockSpec((BATCH_TILE, tseq, D_MODEL), lambda g, s: (g, s, 0))
    return pl.pallas_call(
        functools.partial(_mixer_body, tseq=tseq),
        grid=(batch // BATCH_TILE, seq // tseq),
        in_specs=[
            x_spec, conv_spec, state_spec, state_spec,
            layer_block(1, D_MODEL),
            layer_block(D_MODEL, IN_PROJ_WIDTH),
            layer_block(CONV_K, CONV_CH),
            layer_block(1, SSM_LANES),
            layer_block(1, SSM_LANES),
            layer_block(N_PAIRS, V7X_MXU_DIM, 2 * V7X_MXU_DIM),
            layer_block(N_PAIRS, V7X_MXU_DIM, V7X_MXU_DIM),
            layer_block(1, SSM_WIDTH),
            layer_block(SSM_WIDTH, SSM_WIDTH),
            layer_block(1, CONV_CH),
            layer_block(1, SSM_WIDTH),
            layer_block(D_MODEL, D_MODEL),
        ],
        out_specs=[x_spec, conv_spec, state_spec, state_spec],
        out_shape=[
            jax.ShapeDtypeStruct(x.shape, F32),
            jax.ShapeDtypeStruct(conv0.shape, F32),
            jax.ShapeDtypeStruct(h0_re.shape, F32),
            jax.ShapeDtypeStruct(h0_im.shape, F32),
        ],
        scratch_shapes=[
            pltpu.VMEM(((CONV_K - 1) * BATCH_TILE, CONV_CH), F32),
            pltpu.VMEM((BATCH_TILE, SSM_LANES), F32),
            pltpu.VMEM((BATCH_TILE, SSM_LANES), F32),
            pltpu.VMEM((LANE_SLABS, BATCH_TILE * _turn_pitch(tseq), V7X_LANES), F32),
        ],
        compiler_params=pltpu.CompilerParams(
            dimension_semantics=("arbitrary", "arbitrary"), vmem_limit_bytes=VMEM_LIMIT),
        name=f"mixer_l{layer}_s{seq}",
    )(x, conv0, h0_re, h0_im, lw["g_mix"], lw["w_in"], lw["conv_w"], lw["a8_re"], lw["a8_im"], lw["w1"], lw["w2"],
      lw["d_skip"], lw["w_glu"], lw["g_grp_a"], lw["g_grp_b"], lw["w_out"])


def _attend_body(x_ref, k_ref, v_ref, g_ref, wq_ref, wo_ref, o_ref, *, bt, tq):
    x = x_ref[...].reshape(bt * tq, D_MODEL)
    xn = _rmsnorm(x, g_ref[...]).astype(BF16)
    q = (_dot(xn, wq_ref[...]) * (MEM_HEAD_DIM ** -0.5)).astype(BF16)
    seqs = []
    for i in range(bt):
        rows = slice(i * tq, (i + 1) * tq)
        k = k_ref[i].astype(BF16)
        v = v_ref[i].astype(BF16)
        heads = []
        for h in range(MEM_HEADS):
            lanes = slice(h * MEM_HEAD_DIM, (h + 1) * MEM_HEAD_DIM)
            s = lax.dot_general(q[rows, lanes], k[:, lanes], (((1,), (1,)), ((), ())), preferred_element_type=F32)
            p = jnp.exp(s - jnp.max(s, axis=-1, keepdims=True))
            o = _dot(p.astype(BF16), v[:, lanes]) / jnp.sum(p, axis=-1, keepdims=True)
            heads.append(o.astype(BF16))
        seqs.append(jnp.concatenate(heads, axis=1))
    out = x + _dot(jnp.concatenate(seqs, axis=0), wo_ref[...])
    o_ref[...] = out.reshape(bt, tq, D_MODEL)


def _attend(x, mem_k, mem_v, lw, layer):
    batch, seq, _ = x.shape
    tq = min(ATTEND_SEQ_TILE, seq)
    bt = min(max(ROW_TILE // tq, 1), batch)
    assert seq % tq == 0 and batch % bt == 0
    kv_spec = pl.BlockSpec((None, bt, N_MEM, D_MODEL), lambda b, s: (layer, b, 0, 0))
    x_spec = pl.BlockSpec((bt, tq, D_MODEL), lambda b, s: (b, s, 0))
    return pl.pallas_call(
        functools.partial(_attend_body, bt=bt, tq=tq),
        grid=(batch // bt, seq // tq),
        in_specs=[
            x_spec,
            kv_spec,
            kv_spec,
            _resident((None, 1, D_MODEL), lambda b, s: (layer, 0, 0)),
            _resident((None, D_MODEL, D_MODEL), lambda b, s: (layer, 0, 0)),
            _resident((None, D_MODEL, D_MODEL), lambda b, s: (layer, 0, 0)),
        ],
        out_specs=x_spec,
        out_shape=jax.ShapeDtypeStruct(x.shape, F32),
        compiler_params=pltpu.CompilerParams(
            dimension_semantics=("arbitrary", "arbitrary"), vmem_limit_bytes=VMEM_LIMIT),
        name=f"attend_l{layer}_s{seq}",
    )(x, mem_k, mem_v, lw["g_xattn"], lw["w_q"], lw["w_o"])


def _mlp_body(x_ref, g_ref, wup_ref, wdown_ref, gfinal_ref, o_ref, *, final_norm):
    x = x_ref[...]
    xn = _rmsnorm(x, g_ref[...]).astype(BF16)
    out = x
    hid = D_FF // MLP_HIDDEN_PARTS
    for c in range(MLP_HIDDEN_PARTS):
        hdn = jnp.square(jnp.maximum(_dot(xn, wup_ref[:, c * hid:(c + 1) * hid]), 0.0)).astype(BF16)
        out = out + _dot(hdn, wdown_ref[c * hid:(c + 1) * hid, :])
    if final_norm:
        out = _rmsnorm(out, gfinal_ref[...])
    o_ref[...] = out


def _mlp(x, lw, g_final, layer, final_norm):
    shape = x.shape
    x2 = x.reshape(-1, D_MODEL)
    n_rows = x2.shape[0]
    tm = min(MLP_ROW_TILE, n_rows)
    assert n_rows % tm == 0
    out = pl.pallas_call(
        functools.partial(_mlp_body, final_norm=final_norm),
        grid=(n_rows // tm,),
        in_specs=[
            pl.BlockSpec((tm, D_MODEL), lambda i: (i, 0)),
            _resident((None, 1, D_MODEL), lambda i: (layer, 0, 0)),
            _resident((None, D_MODEL, D_FF), lambda i: (layer, 0, 0)),
            _resident((None, D_FF, D_MODEL), lambda i: (layer, 0, 0)),
            _resident((1, D_MODEL), lambda i: (0, 0)),
        ],
        out_specs=pl.BlockSpec((tm, D_MODEL), lambda i: (i, 0)),
        out_shape=jax.ShapeDtypeStruct(x2.shape, F32),
        compiler_params=pltpu.CompilerParams(
            dimension_semantics=("arbitrary",), vmem_limit_bytes=VMEM_LIMIT),
        name=f"mlp_l{layer}_r{n_rows}",
    )(x2, lw["g_mlp"], lw["w_up"], lw["w_down"], g_final)
    return out.reshape(shape)


def _memkv_body(mem_ref, g_ref, wk_ref, wv_ref, k_ref, v_ref, kb_ref, vb_ref):
    mn = _rmsnorm(mem_ref[...].reshape(MEMKV_BATCH_TILE * N_MEM, D_MODEL), g_ref[...]).astype(BF16)
    for w_ref, heads_ref, flat_ref in ((wk_ref, k_ref, kb_ref), (wv_ref, v_ref, vb_ref)):
        kv = _dot(mn, w_ref[...])
        flat_ref[...] = kv.reshape(MEMKV_BATCH_TILE, N_MEM, D_MODEL).astype(BF16)
        for h in range(MEM_HEADS):
            heads_ref[:, :, h, :] = kv[:, h * MEM_HEAD_DIM:(h + 1) * MEM_HEAD_DIM].reshape(
                MEMKV_BATCH_TILE, N_MEM, MEM_HEAD_DIM)


def _memkv(mem, g_mem, w_k, w_v):
    batch = mem.shape[0]
    depth = w_k.shape[0]
    assert batch % MEMKV_BATCH_TILE == 0
    heads_shape = jax.ShapeDtypeStruct((depth, batch, N_MEM, MEM_HEADS, MEM_HEAD_DIM), F32)
    heads_spec = pl.BlockSpec((None, MEMKV_BATCH_TILE, N_MEM, MEM_HEADS, MEM_HEAD_DIM), lambda l, b: (l, b, 0, 0, 0))
    flat_shape = jax.ShapeDtypeStruct((depth, batch, N_MEM, D_MODEL), BF16)
    flat_spec = pl.BlockSpec((None, MEMKV_BATCH_TILE, N_MEM, D_MODEL), lambda l, b: (l, b, 0, 0))
    w_spec = pl.BlockSpec((None, D_MODEL, D_MODEL), lambda l, b: (l, 0, 0))
    return pl.pallas_call(
        _memkv_body,
        grid=(depth, batch // MEMKV_BATCH_TILE),
        in_specs=[
            pl.BlockSpec((MEMKV_BATCH_TILE, N_MEM, D_MODEL), lambda l, b: (b, 0, 0)),
            pl.BlockSpec((None, 1, D_MODEL), lambda l, b: (l, 0, 0)),
            w_spec,
            w_spec,
        ],
        out_specs=[heads_spec, heads_spec, flat_spec, flat_spec],
        out_shape=[heads_shape, heads_shape, flat_shape, flat_shape],
        compiler_params=pltpu.CompilerParams(
            dimension_semantics=("arbitrary", "arbitrary"), vmem_limit_bytes=VMEM_LIMIT),
        name="memkv",
    )(mem, g_mem, w_k, w_v)


def _s5_fold_body(are_ref, aim_ref, logdt_ref, bre_ref, bim_ref, cre_ref, cim_ref,
                  w1_ref, w2_ref, a8re_ref, a8im_ref, w1_scr, w2t_scr):
    w1_scr[...] = jnp.zeros_like(w1_scr)
    w2t_scr[...] = jnp.zeros_like(w2t_scr)
    k = lax.broadcasted_iota(jnp.int32, (2 * CHUNK, SSM_STATE), 0).astype(F32)
    for g in range(PAIR_GROUPS):
        lam_re = jnp.minimum(are_ref[g], A_RE_MAX)
        lam_im = aim_ref[g]
        dt = jnp.exp(logdt_ref[g])
        mag = jnp.exp(k * (lam_re * dt))
        ang = k * (lam_im * dt)
        p_re, p_im = mag * jnp.cos(ang), mag * jnp.sin(ang)
        a8re_ref[:, g * SSM_STATE:(g + 1) * SSM_STATE] = p_re[CHUNK:CHUNK + 1]
        a8im_ref[:, g * SSM_STATE:(g + 1) * SSM_STATE] = p_im[CHUNK:CHUNK + 1]
        den = lam_re * lam_re + lam_im * lam_im
        n_re, n_im = p_re[1:2] - 1.0, p_im[1:2]
        f_re = (n_re * lam_re + n_im * lam_im) / den
        f_im = (n_im * lam_re - n_re * lam_im) / den
        b_re, b_im = bre_ref[g], bim_ref[g]
        bb_re, bb_im = f_re * b_re - f_im * b_im, f_re * b_im + f_im * b_re
        c_re, c_im = cre_ref[g], cim_ref[g]
        e_re, e_im = [], []
        for j in range(CHUNK):
            pj_re, pj_im = p_re[j:j + 1], p_im[j:j + 1]
            e_re.append(pj_re * bb_re - pj_im * bb_im)
            e_im.append(pj_re * bb_im + pj_im * bb_re)
            p = CHUNK - 1 - j
            for part, e in enumerate((e_re[j], e_im[j])):
                w1_scr[pl.ds(p * PAIR_U + g * SSM_GROUP, SSM_GROUP),
                       pl.ds(part * PAIR_STATE + g * SSM_STATE, SSM_STATE)] = e
            r_re, r_im = p_re[j + 1:j + 2], p_im[j + 1:j + 2]
            for part, d in enumerate((c_re * r_re - c_im * r_im, -(c_re * r_im + c_im * r_re))):
                w2t_scr[pl.ds(j * PAIR_U + g * SSM_GROUP, SSM_GROUP),
                        pl.ds(part * PAIR_STATE + g * SSM_STATE, SSM_STATE)] = d
        nt = (((1,), (1,)), ((), ()))
        kern = (lax.dot_general(jnp.concatenate(e_re, axis=0), c_re, nt, precision=lax.Precision.HIGHEST,
                                preferred_element_type=F32)
                - lax.dot_general(jnp.concatenate(e_im, axis=0), c_im, nt, precision=lax.Precision.HIGHEST,
                                  preferred_element_type=F32))
        for j in range(CHUNK):
            k_j = kern[j * SSM_GROUP:(j + 1) * SSM_GROUP]
            for p in range(CHUNK - j):
                w1_scr[pl.ds(p * PAIR_U + g * SSM_GROUP, SSM_GROUP),
                       pl.ds(2 * PAIR_STATE + (p + j) * PAIR_U + g * SSM_GROUP, SSM_GROUP)] = k_j
    w1_ref[...] = w1_scr[...].astype(BF16)
    w2_ref[...] = w2t_scr[...].T.astype(BF16)


def _s5_fold(a_re, a_im, log_dt, b_re, b_im, c_re, c_im):
    depth = a_re.shape[0]
    row = lambda m: m.reshape(depth, SSM_GROUPS, 1, SSM_STATE)
    logdt = jnp.broadcast_to(log_dt[..., None, None], (depth, SSM_GROUPS, 1, SSM_STATE))
    vec_spec = pl.BlockSpec((None, PAIR_GROUPS, 1, SSM_STATE), lambda l, n: (l, n, 0, 0))
    mat_spec = pl.BlockSpec((None, PAIR_GROUPS, SSM_GROUP, SSM_STATE), lambda l, n: (l, n, 0, 0))
    a8_spec = pl.BlockSpec((None, 1, PAIR_STATE), lambda l, n: (l, 0, n))
    a8_shape = jax.ShapeDtypeStruct((depth, 1, SSM_LANES), F32)
    return pl.pallas_call(
        _s5_fold_body,
        grid=(depth, N_PAIRS),
        in_specs=[vec_spec, vec_spec, vec_spec, mat_spec, mat_spec, mat_spec, mat_spec],
        out_specs=[
            pl.BlockSpec((None, None, V7X_MXU_DIM, 2 * V7X_MXU_DIM), lambda l, n: (l, n, 0, 0)),
            pl.BlockSpec((None, None, V7X_MXU_DIM, V7X_MXU_DIM), lambda l, n: (l, n, 0, 0)),
            a8_spec, a8_spec,
        ],
        out_shape=[
            jax.ShapeDtypeStruct((depth, N_PAIRS, V7X_MXU_DIM, 2 * V7X_MXU_DIM), BF16),
            jax.ShapeDtypeStruct((depth, N_PAIRS, V7X_MXU_DIM, V7X_MXU_DIM), BF16),
            a8_shape, a8_shape,
        ],
        scratch_shapes=[
            pltpu.VMEM((V7X_MXU_DIM, 2 * V7X_MXU_DIM), F32),
            pltpu.VMEM((V7X_MXU_DIM, V7X_MXU_DIM), F32),
        ],
        compiler_params=pltpu.CompilerParams(dimension_semantics=("arbitrary", "arbitrary")),
        name="s5_fold",
    )(row(a_re), row(a_im), logdt, jnp.swapaxes(b_re, -1, -2), jnp.swapaxes(b_im, -1, -2), c_re, c_im)


def kernel(x_prompt, x_sample, mem_prompt, state_conv, state_ssm_re, state_ssm_im, cache_mem_k, cache_mem_v, g_mix, w_in, conv_w, ssm_a_re, ssm_a_im, ssm_log_dt, ssm_b_re, ssm_b_im, ssm_c_re, ssm_c_im, ssm_d, w_glu, g_grp_a, g_grp_b, w_out, g_xattn, g_mem, w_q, w_k, w_v, w_o, g_mlp, w_up, w_down, g_final):
    depth = w_in.shape[0]
    bp = x_prompt.shape[0]
    bs = x_sample.shape[0]
    w1, w2, a8_re, a8_im = _s5_fold(ssm_a_re, ssm_a_im, ssm_log_dt, ssm_b_re, ssm_b_im, ssm_c_re, ssm_c_im)
    row = lambda g: g.reshape(depth, 1, -1)
    lw = {
        "g_mix": row(g_mix), "w_in": w_in.astype(BF16), "conv_w": conv_w, "a8_re": a8_re, "a8_im": a8_im,
        "w1": w1, "w2": w2, "d_skip": row(ssm_d), "w_glu": w_glu.astype(BF16),
        "g_grp_a": row(g_grp_a), "g_grp_b": row(g_grp_b), "w_out": w_out.astype(BF16),
        "g_xattn": row(g_xattn), "w_q": w_q.astype(BF16), "w_o": w_o.astype(BF16),
        "g_mlp": row(g_mlp), "w_up": w_up.astype(BF16), "w_down": w_down.astype(BF16),
    }
    g_final2 = g_final.reshape(1, D_MODEL)

    mk_out, mv_out, mk_p, mv_p = _memkv(mem_prompt, row(g_mem), w_k.astype(BF16), w_v.astype(BF16))
    mk_s = cache_mem_k.reshape(depth, bs, N_MEM, D_MODEL)
    mv_s = cache_mem_v.reshape(depth, bs, N_MEM, D_MODEL)

    zero_conv = jnp.zeros((CONV_K - 1, bp, CONV_CH), F32)
    zero_h = jnp.zeros((bp, SSM_LANES), F32)
    state_shape = (-1, SSM_GROUPS, SSM_STATE)

    xp, xs = x_prompt, x_sample
    conv_p, re_p, im_p, conv_s, re_s, im_s = [], [], [], [], [], []
    for l in range(depth):
        last = l == depth - 1
        xp, c_new, r_new, i_new = _mixer(xp, zero_conv, zero_h, zero_h, lw, l)
        conv_p.append(jnp.swapaxes(c_new, 0, 1))
        re_p.append(r_new.reshape(state_shape))
        im_p.append(i_new.reshape(state_shape))
        xp = _attend(xp, mk_p, mv_p, lw, l)
        xp = _mlp(xp, lw, g_final2, l, last)

        xs, c_new, r_new, i_new = _mixer(xs, jnp.swapaxes(state_conv[l], 0, 1),
                                         state_ssm_re[l].reshape(bs, SSM_LANES),
                                         state_ssm_im[l].reshape(bs, SSM_LANES), lw, l)
        conv_s.append(jnp.swapaxes(c_new, 0, 1))
        re_s.append(r_new.reshape(state_shape))
        im_s.append(i_new.reshape(state_shape))
        xs = _attend(xs, mk_s, mv_s, lw, l)
        xs = _mlp(xs, lw, g_final2, l, last)

    return (xp, xs,
            jnp.stack(conv_p), jnp.stack(re_p), jnp.stack(im_p), mk_out, mv_out,
            jnp.stack(conv_s), jnp.stack(re_s), jnp.stack(im_s))
```

```python
import functools

import jax
import jax.numpy as jnp
from jax import lax
from jax.experimental import pallas as pl
from jax.experimental.pallas import tpu as pltpu

D_MODEL = 1024
CONV_CH = 512
CONV_K = 3
SSM_WIDTH = 512
SSM_GROUP = 16
SSM_GROUPS = 32
SSM_STATE = 64
N_MEM = 256
MEM_HEADS = 4
MEM_HEAD_DIM = D_MODEL // MEM_HEADS
D_FF = 4 * D_MODEL
IN_PROJ_WIDTH = 3 * CONV_CH + SSM_WIDTH
EPS = 1e-6
A_RE_MAX = -1e-4

F32 = jnp.float32
BF16 = jnp.bfloat16

V7X_F32_SUBLANES = 8
V7X_LANES = 128
V7X_MXU_DIM = 256
V7X_VMEM_BYTES = 64 * 1024 * 1024

PAIR_GROUPS = V7X_MXU_DIM // (2 * SSM_STATE)
PAIR_U = PAIR_GROUPS * SSM_GROUP
PAIR_STATE = PAIR_GROUPS * SSM_STATE
N_PAIRS = SSM_GROUPS // PAIR_GROUPS
CHUNK = V7X_MXU_DIM // PAIR_U
SSM_LANES = SSM_GROUPS * SSM_STATE

LANE_SLABS = D_MODEL // V7X_LANES
BATCH_TILE = V7X_F32_SUBLANES
SEQ_TILE = 128
HEAD_PARTS = 2
ROW_TILE = 512
MLP_ROW_TILE = 1024
MLP_HIDDEN_PARTS = 2
ATTEND_SEQ_TILE = 1024
MEMKV_BATCH_TILE = 2
FOLD_PAIRS = 4
VMEM_LIMIT = V7X_VMEM_BYTES - 8 * 1024 * 1024


def _rmsnorm(x, g):
    inv = lax.rsqrt(jnp.mean(x * x, axis=-1, keepdims=True) + EPS)
    return x * inv * g


def _dot(a, b):
    return jnp.dot(a, b, preferred_element_type=F32)


def _turn_pitch(tseq):
    assert tseq % V7X_F32_SUBLANES == 0
    return tseq + V7X_F32_SUBLANES // 2


def _resident(block_shape, index_map):
    return pl.BlockSpec(block_shape, index_map, pipeline_mode=pl.Buffered(1))


def _s5_regroup_in(u, n_chunks):
    crow = n_chunks * BATCH_TILE
    u_pos = u.reshape(n_chunks, CHUNK, BATCH_TILE, SSM_WIDTH)
    u_pos = [u_pos[:, p].reshape(crow, SSM_WIDTH) for p in range(CHUNK)]
    return [jnp.concatenate([up[:, gp * PAIR_U:(gp + 1) * PAIR_U] for up in u_pos], axis=1).astype(BF16)
            for gp in range(N_PAIRS)]


def _s5_chunk_inputs(xg, w1_ref):
    zy = [_dot(xg[gp], w1_ref[gp]) for gp in range(N_PAIRS)]
    z_re = jnp.concatenate([m[:, 0:PAIR_STATE] for m in zy], axis=1)
    z_im = jnp.concatenate([m[:, PAIR_STATE:2 * PAIR_STATE] for m in zy], axis=1)
    return z_re, z_im, [m[:, 2 * PAIR_STATE:] for m in zy]


def _s5_scan(z_re, z_im, hre_scr, him_scr, a8r_ref, a8i_ref, n_chunks):
    a_re = jnp.broadcast_to(a8r_ref[...], (BATCH_TILE, SSM_LANES))
    a_im = jnp.broadcast_to(a8i_ref[...], (BATCH_TILE, SSM_LANES))
    h_re, h_im = hre_scr[...], him_scr[...]
    prev_re, prev_im = [], []
    for c in range(n_chunks):
        prev_re.append(h_re)
        prev_im.append(h_im)
        seqs = slice(c * BATCH_TILE, (c + 1) * BATCH_TILE)
        h_re, h_im = (a_re * h_re - a_im * h_im + z_re[seqs], a_re * h_im + a_im * h_re + z_im[seqs])
    hre_scr[...] = h_re
    him_scr[...] = h_im
    return jnp.concatenate(prev_re, axis=0), jnp.concatenate(prev_im, axis=0)


def _s5_chunk_outputs(y_in, prev_re, prev_im, w2_ref):
    y_pair = []
    for gp in range(N_PAIRS):
        lanes = slice(gp * PAIR_STATE, (gp + 1) * PAIR_STATE)
        sp = jnp.concatenate([prev_re[:, lanes], prev_im[:, lanes]], axis=1).astype(BF16)
        y_pair.append((y_in[gp] + _dot(sp, w2_ref[gp])).astype(BF16))
    return y_pair


def _s5_regroup_out(y_pair, n_chunks):
    y_pos = [jnp.concatenate([yp[:, q * PAIR_U:(q + 1) * PAIR_U] for yp in y_pair], axis=1).astype(F32)
             .reshape(n_chunks, BATCH_TILE, SSM_WIDTH) for q in range(CHUNK)]
    return jnp.stack(y_pos, axis=1).reshape(n_chunks * CHUNK * BATCH_TILE, SSM_WIDTH)


def _mixer_body(x_ref, conv0_ref, hre0_ref, him0_ref, gmix_ref, win_ref, convw_ref, a8r_ref, a8i_ref, w1_ref, w2_ref,
                dskip_ref, wglu_ref, ga_ref, gb_ref, wout_ref,
                o_ref, convo_ref, hreo_ref, himo_ref,
                conv_scr, hre_scr, him_scr, turn_scr, *, tseq):
    rows = BATCH_TILE * tseq

    @pl.when(pl.program_id(1) == 0)
    def _():
        conv_scr[...] = conv0_ref[...].reshape((CONV_K - 1) * BATCH_TILE, CONV_CH)
        hre_scr[...] = hre0_ref[...]
        him_scr[...] = him0_ref[...]

    pitch = _turn_pitch(tseq)
    gmix = gmix_ref[...]
    n_parts = HEAD_PARTS if tseq == SEQ_TILE else 1
    part = tseq // n_parts
    xn, u = [], []
    for i in range(n_parts):
        for b in range(BATCH_TILE):
            xn_b = _rmsnorm(x_ref[b, i * part:(i + 1) * part], gmix)
            for j in range(LANE_SLABS):
                turn_scr[j, pl.ds(b * pitch + i * part, part), :] = xn_b[:, j * V7X_LANES:(j + 1) * V7X_LANES]
        xn.append(jnp.concatenate(
            [jnp.concatenate([turn_scr[j, pl.ds(t, BATCH_TILE, stride=pitch), :]
                              for t in range(i * part, (i + 1) * part)], axis=0)
             for j in range(LANE_SLABS)], axis=1).astype(BF16))
        u.append(_dot(xn[i], win_ref[:, 3 * CONV_CH:]))
    u = jnp.concatenate(u, axis=0)

    def in_proj(block):
        cols = slice(block * CONV_CH, (block + 1) * CONV_CH)
        return jnp.concatenate([_dot(xn_i, win_ref[:, cols]) for xn_i in xn], axis=0)

    n_chunks = tseq // CHUNK
    xg = _s5_regroup_in(u, n_chunks)
    c_gate = in_proj(1)
    z_re, z_im, y_in = _s5_chunk_inputs(xg, w1_ref)
    v = in_proj(2)
    prev_re, prev_im = _s5_scan(z_re, z_im, hre_scr, him_scr, a8r_ref, a8i_ref, n_chunks)

    xin = c_gate * v
    xp = jnp.concatenate([conv_scr[...], xin], axis=0)
    w = convw_ref[...]
    conv = w[0:1] * xp[0:rows]
    for k in range(1, CONV_K):
        conv = conv + w[k:k + 1] * xp[k * BATCH_TILE:k * BATCH_TILE + rows]
    conv_scr[...] = xp[rows:]

    y_pair = _s5_chunk_outputs(y_in, prev_re, prev_im, w2_ref)
    b_gate = in_proj(0)
    y_s = _s5_regroup_out(y_pair, n_chunks) + dskip_ref[...] * u
    y_a = _rmsnorm(b_gate * conv, ga_ref[...]).astype(BF16)
    y_s = jax.nn.gelu(y_s)
    gate = _dot(y_s.astype(BF16), wglu_ref[...])
    upd_a = _dot(y_a, wout_ref[0:CONV_CH, :])
    y_b = _rmsnorm(y_s * jax.nn.sigmoid(gate), gb_ref[...]).astype(BF16)

    slabs_per_block = V7X_MXU_DIM // V7X_LANES
    for n in range(D_MODEL // V7X_MXU_DIM):
        cols = slice(n * V7X_MXU_DIM, (n + 1) * V7X_MXU_DIM)
        upd = upd_a[:, cols] + _dot(y_b, wout_ref[CONV_CH:, cols])
        for k in range(slabs_per_block):
            j = n * slabs_per_block + k
            for t in range(tseq):
                turn_scr[j, pl.ds(t, BATCH_TILE, stride=pitch), :] = upd[t * BATCH_TILE:(t + 1) * BATCH_TILE,
                                                                         k * V7X_LANES:(k + 1) * V7X_LANES]
        for b in range(BATCH_TILE):
            o_ref[b, :, cols] = x_ref[b, :, cols] + jnp.concatenate(
                [turn_scr[n * slabs_per_block + k, pl.ds(b * pitch, tseq), :] for k in range(slabs_per_block)], axis=1)
    convo_ref[...] = conv_scr[...].reshape(CONV_K - 1, BATCH_TILE, CONV_CH)
    hreo_ref[...] = hre_scr[...]
    himo_ref[...] = him_scr[...]


def _mixer(x, conv0, h0_re, h0_im, lw, layer):
    batch, seq, _ = x.shape
    tseq = min(SEQ_TILE, seq)
    assert batch % BATCH_TILE == 0 and seq % tseq == 0 and tseq % CHUNK == 0

    def layer_block(*shape):
        return _resident((None,) + shape, lambda g, s: (layer,) + (0,) * len(shape))

    state_spec = pl.BlockSpec((BATCH_TILE, SSM_LANES), lambda g, s: (g, 0))
    conv_spec = pl.BlockSpec((CONV_K - 1, BATCH_TILE, CONV_CH), lambda g, s: (0, g, 0))
    x_spec = pl.BlockSpec((BATCH_TILE, tseq, D_MODEL), lambda g, s: (g, s, 0))
    return pl.pallas_call(
        functools.partial(_mixer_body, tseq=tseq),
        grid=(batch // BATCH_TILE, seq // tseq),
        in_specs=[
            x_spec, conv_spec, state_spec, state_spec,
            layer_block(1, D_MODEL),
            layer_block(D_MODEL, IN_PROJ_WIDTH),
            layer_block(CONV_K, CONV_CH),
            layer_block(1, SSM_LANES),
            layer_block(1, SSM_LANES),
            layer_block(N_PAIRS, V7X_MXU_DIM, 2 * V7X_MXU_DIM),
            layer_block(N_PAIRS, V7X_MXU_DIM, V7X_MXU_DIM),
            layer_block(1, SSM_WIDTH),
            layer_block(SSM_WIDTH, SSM_WIDTH),
            layer_block(1, CONV_CH),
            layer_block(1, SSM_WIDTH),
            layer_block(D_MODEL, D_MODEL),
        ],
        out_specs=[x_spec, conv_spec, state_spec, state_spec],
        out_shape=[
            jax.ShapeDtypeStruct(x.shape, F32),
            jax.ShapeDtypeStruct(conv0.shape, F32),
            jax.ShapeDtypeStruct(h0_re.shape, F32),
            jax.ShapeDtypeStruct(h0_im.shape, F32),
        ],
        scratch_shapes=[
            pltpu.VMEM(((CONV_K - 1) * BATCH_TILE, CONV_CH), F32),
            pltpu.VMEM((BATCH_TILE, SSM_LANES), F32),
            pltpu.VMEM((BATCH_TILE, SSM_LANES), F32),
            pltpu.VMEM((LANE_SLABS, BATCH_TILE * _turn_pitch(tseq), V7X_LANES), F32),
        ],
        compiler_params=pltpu.CompilerParams(
            dimension_semantics=("arbitrary", "arbitrary"), vmem_limit_bytes=VMEM_LIMIT),
        name=f"mixer_l{layer}_s{seq}",
    )(x, conv0, h0_re, h0_im, lw["g_mix"], lw["w_in"], lw["conv_w"], lw["a8_re"], lw["a8_im"], lw["w1"], lw["w2"],
      lw["d_skip"], lw["w_glu"], lw["g_grp_a"], lw["g_grp_b"], lw["w_out"])


def _attend_body(x_ref, k_ref, v_ref, g_ref, wq_ref, wo_ref, o_ref, *, bt, tq):
    x = x_ref[...].reshape(bt * tq, D_MODEL)
    xn = _rmsnorm(x, g_ref[...]).astype(BF16)
    q = (_dot(xn, wq_ref[...]) * (MEM_HEAD_DIM ** -0.5)).astype(BF16)
    seqs = []
    for i in range(bt):
        rows = slice(i * tq, (i + 1) * tq)
        k = k_ref[i].astype(BF16)
        v = v_ref[i].astype(BF16)
        heads = []
        for h in range(MEM_HEADS):
            lanes = slice(h * MEM_HEAD_DIM, (h + 1) * MEM_HEAD_DIM)
            s = lax.dot_general(q[rows, lanes], k[:, lanes], (((1,), (1,)), ((), ())), preferred_element_type=F32)
            p = jnp.exp(s - jnp.max(s, axis=-1, keepdims=True))
            o = _dot(p.astype(BF16), v[:, lanes]) / jnp.sum(p, axis=-1, keepdims=True)
            heads.append(o.astype(BF16))
        seqs.append(jnp.concatenate(heads, axis=1))
    out = x + _dot(jnp.concatenate(seqs, axis=0), wo_ref[...])
    o_ref[...] = out.reshape(bt, tq, D_MODEL)


def _attend(x, mem_k, mem_v, lw, layer):
    batch, seq, _ = x.shape
    tq = min(ATTEND_SEQ_TILE, seq)
    bt = min(max(ROW_TILE // tq, 1), batch)
    assert seq % tq == 0 and batch % bt == 0
    kv_spec = pl.BlockSpec((None, bt, N_MEM, D_MODEL), lambda b, s: (layer, b, 0, 0))
    x_spec = pl.BlockSpec((bt, tq, D_MODEL), lambda b, s: (b, s, 0))
    return pl.pallas_call(
        functools.partial(_attend_body, bt=bt, tq=tq),
        grid=(batch // bt, seq // tq),
        in_specs=[
            x_spec,
            kv_spec,
            kv_spec,
            _resident((None, 1, D_MODEL), lambda b, s: (layer, 0, 0)),
            _resident((None, D_MODEL, D_MODEL), lambda b, s: (layer, 0, 0)),
            _resident((None, D_MODEL, D_MODEL), lambda b, s: (layer, 0, 0)),
        ],
        out_specs=x_spec,
        out_shape=jax.ShapeDtypeStruct(x.shape, F32),
        compiler_params=pltpu.CompilerParams(
            dimension_semantics=("arbitrary", "arbitrary"), vmem_limit_bytes=VMEM_LIMIT),
        name=f"attend_l{layer}_s{seq}",
    )(x, mem_k, mem_v, lw["g_xattn"], lw["w_q"], lw["w_o"])


def _mlp_body(x_ref, g_ref, wup_ref, wdown_ref, gfinal_ref, o_ref, *, final_norm):
    x = x_ref[...]
    xn = _rmsnorm(x, g_ref[...]).astype(BF16)
    out = x
    hid = D_FF // MLP_HIDDEN_PARTS
    for c in range(MLP_HIDDEN_PARTS):
        hdn = jnp.square(jnp.maximum(_dot(xn, wup_ref[:, c * hid:(c + 1) * hid]), 0.0)).astype(BF16)
        out = out + _dot(hdn, wdown_ref[c * hid:(c + 1) * hid, :])
    if final_norm:
        out = _rmsnorm(out, gfinal_ref[...])
    o_ref[...] = out


def _mlp(x, lw, g_final, layer, final_norm):
    shape = x.shape
    x2 = x.reshape(-1, D_MODEL)
    n_rows = x2.shape[0]
    tm = min(MLP_ROW_TILE, n_rows)
    assert n_rows % tm == 0
    out = pl.pallas_call(
        functools.partial(_mlp_body, final_norm=final_norm),
        grid=(n_rows // tm,),
        in_specs=[
            pl.BlockSpec((tm, D_MODEL), lambda i: (i, 0)),
            _resident((None, 1, D_MODEL), lambda i: (layer, 0, 0)),
            _resident((None, D_MODEL, D_FF), lambda i: (layer, 0, 0)),
            _resident((None, D_FF, D_MODEL), lambda i: (layer, 0, 0)),
            _resident((1, D_MODEL), lambda i: (0, 0)),
        ],
        out_specs=pl.BlockSpec((tm, D_MODEL), lambda i: (i, 0)),
        out_shape=jax.ShapeDtypeStruct(x2.shape, F32),
        compiler_params=pltpu.CompilerParams(
            dimension_semantics=("arbitrary",), vmem_limit_bytes=VMEM_LIMIT),
        name=f"mlp_l{layer}_r{n_rows}",
    )(x2, lw["g_mlp"], lw["w_up"], lw["w_down"], g_final)
    return out.reshape(shape)


def _memkv_body(mem_ref, g_ref, wk_ref, wv_ref, k_ref, v_ref, kb_ref, vb_ref):
    mn = _rmsnorm(mem_ref[...].reshape(MEMKV_BATCH_TILE * N_MEM, D_MODEL), g_ref[...]).astype(BF16)
    for w_ref, heads_ref, flat_ref in ((wk_ref, k_ref, kb_ref), (wv_ref, v_ref, vb_ref)):
        kv = _dot(mn, w_ref[...])
        flat_ref[...] = kv.reshape(MEMKV_BATCH_TILE, N_MEM, D_MODEL).astype(BF16)
        for h in range(MEM_HEADS):
            heads_ref[:, :, h, :] = kv[:, h * MEM_HEAD_DIM:(h + 1) * MEM_HEAD_DIM].reshape(
                MEMKV_BATCH_TILE, N_MEM, MEM_HEAD_DIM)


def _memkv(mem, g_mem, w_k, w_v):
    batch = mem.shape[0]
    depth = w_k.shape[0]
    assert batch % MEMKV_BATCH_TILE == 0
    heads_shape = jax.ShapeDtypeStruct((depth, batch, N_MEM, MEM_HEADS, MEM_HEAD_DIM), F32)
    heads_spec = pl.BlockSpec((None, MEMKV_BATCH_TILE, N_MEM, MEM_HEADS, MEM_HEAD_DIM), lambda l, b: (l, b, 0, 0, 0))
    flat_shape = jax.ShapeDtypeStruct((depth, batch, N_MEM, D_MODEL), BF16)
    flat_spec = pl.BlockSpec((None, MEMKV_BATCH_TILE, N_MEM, D_MODEL), lambda l, b: (l, b, 0, 0))
    w_spec = pl.BlockSpec((None, D_MODEL, D_MODEL), lambda l, b: (l, 0, 0))
    return pl.pallas_call(
        _memkv_body,
        grid=(depth, batch // MEMKV_BATCH_TILE),
        in_specs=[
            pl.BlockSpec((MEMKV_BATCH_TILE, N_MEM, D_MODEL), lambda l, b: (b, 0, 0)),
            pl.BlockSpec((None, 1, D_MODEL), lambda l, b: (l, 0, 0)),
            w_spec,
            w_spec,
        ],
        out_specs=[heads_spec, heads_spec, flat_spec, flat_spec],
        out_shape=[heads_shape, heads_shape, flat_shape, flat_shape],
        compiler_params=pltpu.CompilerParams(
            dimension_semantics=("arbitrary", "arbitrary"), vmem_limit_bytes=VMEM_LIMIT),
        name="memkv",
    )(mem, g_mem, w_k, w_v)


def _s5_fold_body(are_ref, aim_ref, logdt_ref, bre_ref, bim_ref, cre_ref, cim_ref,
                  w1_ref, w2_ref, a8re_ref, a8im_ref, w1_scr, w2t_scr):
    for i in range(FOLD_PAIRS):
        groups = pl.ds(i * PAIR_GROUPS, PAIR_GROUPS)
        lanes = pl.ds(i * PAIR_STATE, PAIR_STATE)
        _s5_fold_pair(*(r.at[groups] for r in (are_ref, aim_ref, logdt_ref, bre_ref, bim_ref, cre_ref, cim_ref)),
                      w1_ref.at[i], w2_ref.at[i], a8re_ref.at[:, lanes], a8im_ref.at[:, lanes], w1_scr, w2t_scr)


def _s5_fold_pair(are_ref, aim_ref, logdt_ref, bre_ref, bim_ref, cre_ref, cim_ref,
                  w1_ref, w2_ref, a8re_ref, a8im_ref, w1_scr, w2t_scr):
    w1_scr[...] = jnp.zeros_like(w1_scr)
    w2t_scr[...] = jnp.zeros_like(w2t_scr)
    k = lax.broadcasted_iota(jnp.int32, (2 * CHUNK, SSM_STATE), 0).astype(F32)
    for g in range(PAIR_GROUPS):
        lam_re = jnp.minimum(are_ref[g], A_RE_MAX)
        lam_im = aim_ref[g]
        dt = jnp.exp(logdt_ref[g])
        mag = jnp.exp(k * (lam_re * dt))
        ang = k * (lam_im * dt)
        p_re, p_im = mag * jnp.cos(ang), mag * jnp.sin(ang)
        a8re_ref[:, g * SSM_STATE:(g + 1) * SSM_STATE] = p_re[CHUNK:CHUNK + 1]
        a8im_ref[:, g * SSM_STATE:(g + 1) * SSM_STATE] = p_im[CHUNK:CHUNK + 1]
        den = lam_re * lam_re + lam_im * lam_im
        n_re, n_im = p_re[1:2] - 1.0, p_im[1:2]
        f_re = (n_re * lam_re + n_im * lam_im) / den
        f_im = (n_im * lam_re - n_re * lam_im) / den
        b_re, b_im = bre_ref[g], bim_ref[g]
        bb_re, bb_im = f_re * b_re - f_im * b_im, f_re * b_im + f_im * b_re
        c_re, c_im = cre_ref[g], cim_ref[g]
        e_re, e_im = [], []
        for j in range(CHUNK):
            pj_re, pj_im = p_re[j:j + 1], p_im[j:j + 1]
            e_re.append(pj_re * bb_re - pj_im * bb_im)
            e_im.append(pj_re * bb_im + pj_im * bb_re)
            p = CHUNK - 1 - j
            for part, e in enumerate((e_re[j], e_im[j])):
                w1_scr[pl.ds(p * PAIR_U + g * SSM_GROUP, SSM_GROUP),
                       pl.ds(part * PAIR_STATE + g * SSM_STATE, SSM_STATE)] = e
            r_re, r_im = p_re[j + 1:j + 2], p_im[j + 1:j + 2]
            for part, d in enumerate((c_re * r_re - c_im * r_im, -(c_re * r_im + c_im * r_re))):
                w2t_scr[pl.ds(j * PAIR_U + g * SSM_GROUP, SSM_GROUP),
                        pl.ds(part * PAIR_STATE + g * SSM_STATE, SSM_STATE)] = d
        nt = (((1,), (1,)), ((), ()))
        kern = (lax.dot_general(jnp.concatenate(e_re, axis=0), c_re, nt, precision=lax.Precision.HIGHEST,
                                preferred_element_type=F32)
                - lax.dot_general(jnp.concatenate(e_im, axis=0), c_im, nt, precision=lax.Precision.HIGHEST,
                                  preferred_element_type=F32))
        for j in range(CHUNK):
            k_j = kern[j * SSM_GROUP:(j + 1) * SSM_GROUP]
            for p in range(CHUNK - j):
                w1_scr[pl.ds(p * PAIR_U + g * SSM_GROUP, SSM_GROUP),
                       pl.ds(2 * PAIR_STATE + (p + j) * PAIR_U + g * SSM_GROUP, SSM_GROUP)] = k_j
    w1_ref[...] = w1_scr[...].astype(BF16)
    w2_ref[...] = w2t_scr[...].T.astype(BF16)


def _s5_fold(a_re, a_im, log_dt, b_re, b_im, c_re, c_im):
    depth = a_re.shape[0]
    row = lambda m: m.reshape(depth, SSM_GROUPS, 1, SSM_STATE)
    logdt = jnp.broadcast_to(log_dt[..., None, None], (depth, SSM_GROUPS, 1, SSM_STATE))
    vec_spec = pl.BlockSpec((None, FOLD_PAIRS * PAIR_GROUPS, 1, SSM_STATE), lambda l, n: (l, n, 0, 0))
    mat_spec = pl.BlockSpec((None, FOLD_PAIRS * PAIR_GROUPS, SSM_GROUP, SSM_STATE), lambda l, n: (l, n, 0, 0))
    a8_spec = pl.BlockSpec((None, 1, FOLD_PAIRS * PAIR_STATE), lambda l, n: (l, 0, n))
    a8_shape = jax.ShapeDtypeStruct((depth, 1, SSM_LANES), F32)
    return pl.pallas_call(
        _s5_fold_body,
        grid=(depth, N_PAIRS // FOLD_PAIRS),
        in_specs=[vec_spec, vec_spec, vec_spec, mat_spec, mat_spec, mat_spec, mat_spec],
        out_specs=[
            pl.BlockSpec((None, FOLD_PAIRS, V7X_MXU_DIM, 2 * V7X_MXU_DIM), lambda l, n: (l, n, 0, 0)),
            pl.BlockSpec((None, FOLD_PAIRS, V7X_MXU_DIM, V7X_MXU_DIM), lambda l, n: (l, n, 0, 0)),
            a8_spec, a8_spec,
        ],
        out_shape=[
            jax.ShapeDtypeStruct((depth, N_PAIRS, V7X_MXU_DIM, 2 * V7X_MXU_DIM), BF16),
            jax.ShapeDtypeStruct((depth, N_PAIRS, V7X_MXU_DIM, V7X_MXU_DIM), BF16),
            a8_shape, a8_shape,
        ],
        scratch_shapes=[
            pltpu.VMEM((V7X_MXU_DIM, 2 * V7X_MXU_DIM), F32),
            pltpu.VMEM((V7X_MXU_DIM, V7X_MXU_DIM), F32),
        ],
        compiler_params=pltpu.CompilerParams(dimension_semantics=("arbitrary", "arbitrary")),
        name="s5_fold",
    )(row(a_re), row(a_im), logdt, jnp.swapaxes(b_re, -1, -2), jnp.swapaxes(b_im, -1, -2), c_re, c_im)


def kernel(x_prompt, x_sample, mem_prompt, state_conv, state_ssm_re, state_ssm_im, cache_mem_k, cache_mem_v, g_mix, w_in, conv_w, ssm_a_re, ssm_a_im, ssm_log_dt, ssm_b_re, ssm_b_im, ssm_c_re, ssm_c_im, ssm_d, w_glu, g_grp_a, g_grp_b, w_out, g_xattn, g_mem, w_q, w_k, w_v, w_o, g_mlp, w_up, w_down, g_final):
    depth = w_in.shape[0]
    bp = x_prompt.shape[0]
    bs = x_sample.shape[0]
    w1, w2, a8_re, a8_im = _s5_fold(ssm_a_re, ssm_a_im, ssm_log_dt, ssm_b_re, ssm_b_im, ssm_c_re, ssm_c_im)
    row = lambda g: g.reshape(depth, 1, -1)
    lw = {
        "g_mix": row(g_mix), "w_in": w_in.astype(BF16), "conv_w": conv_w, "a8_re": a8_re, "a8_im": a8_im,
        "w1": w1, "w2": w2, "d_skip": row(ssm_d), "w_glu": w_glu.astype(BF16),
        "g_grp_a": row(g_grp_a), "g_grp_b": row(g_grp_b), "w_out": w_out.astype(BF16),
        "g_xattn": row(g_xattn), "w_q": w_q.astype(BF16), "w_o": w_o.astype(BF16),
        "g_mlp": row(g_mlp), "w_up": w_up.astype(BF16), "w_down": w_down.astype(BF16),
    }
    g_final2 = g_final.reshape(1, D_MODEL)

    mk_out, mv_out, mk_p, mv_p = _memkv(mem_prompt, row(g_mem), w_k.astype(BF16), w_v.astype(BF16))
    mk_s = cache_mem_k.reshape(depth, bs, N_MEM, D_MODEL)
    mv_s = cache_mem_v.reshape(depth, bs, N_MEM, D_MODEL)

    zero_conv = jnp.zeros((CONV_K - 1, bp, CONV_CH), F32)
    zero_h = jnp.zeros((bp, SSM_LANES), F32)
    state_shape = (-1, SSM_GROUPS, SSM_STATE)

    xp, xs = x_prompt, x_sample
    conv_p, re_p, im_p, conv_s, re_s, im_s = [], [], [], [], [], []
    for l in range(depth):
        last = l == depth - 1
        xp, c_new, r_new, i_new = _mixer(xp, zero_conv, zero_h, zero_h, lw, l)
        conv_p.append(jnp.swapaxes(c_new, 0, 1))
        re_p.append(r_new.reshape(state_shape))
        im_p.append(i_new.reshape(state_shape))
        xp = _attend(xp, mk_p, mv_p, lw, l)
        xp = _mlp(xp, lw, g_final2, l, last)

        xs, c_new, r_new, i_new = _mixer(xs, jnp.swapaxes(state_conv[l], 0, 1),
                                         state_ssm_re[l].reshape(bs, SSM_LANES),
                                         state_ssm_im[l].reshape(bs, SSM_LANES), lw, l)
        conv_s.append(jnp.swapaxes(c_new, 0, 1))
        re_s.append(r_new.reshape(state_shape))
        im_s.append(i_new.reshape(state_shape))
        xs = _attend(xs, mk_s, mv_s, lw, l)
        xs = _mlp(xs, lw, g_final2, l, last)

    return (xp, xs,
            jnp.stack(conv_p), jnp.stack(re_p), jnp.stack(im_p), mk_out, mv_out,
            jnp.stack(conv_s), jnp.stack(re_s), jnp.stack(im_s))
```

```python
import functools

import jax
import jax.numpy as jnp
from jax import lax
from jax.experimental import pallas as pl
from jax.experimental.pallas import tpu as pltpu

D_MODEL = 1024
CONV_CH = 512
CONV_K = 3
SSM_WIDTH = 512
SSM_GROUP = 16
SSM_GROUPS = 32
SSM_STATE = 64
N_MEM = 256
MEM_HEADS = 4
MEM_HEAD_DIM = D_MODEL // MEM_HEADS
D_FF = 4 * D_MODEL
IN_PROJ_WIDTH = 3 * CONV_CH + SSM_WIDTH
EPS = 1e-6
A_RE_MAX = -1e-4

F32 = jnp.float32
BF16 = jnp.bfloat16

V7X_F32_SUBLANES = 8
V7X_LANES = 128
V7X_MXU_DIM = 256
V7X_VMEM_BYTES = 64 * 1024 * 1024

PAIR_GROUPS = V7X_MXU_DIM // (2 * SSM_STATE)
PAIR_U = PAIR_GROUPS * SSM_GROUP
PAIR_STATE = PAIR_GROUPS * SSM_STATE
N_PAIRS = SSM_GROUPS // PAIR_GROUPS
CHUNK = V7X_MXU_DIM // PAIR_U
SSM_LANES = SSM_GROUPS * SSM_STATE

LANE_SLABS = D_MODEL // V7X_LANES
BATCH_TILE = V7X_F32_SUBLANES
SEQ_TILE = 128
HEAD_PARTS = 2
ROW_TILE = 512
MLP_ROW_TILE = 1024
MLP_HIDDEN_PARTS = 2
ATTEND_SEQ_TILE = 2048
MEMKV_BATCH_TILE = 2
FOLD_PAIRS = 4
VMEM_LIMIT = V7X_VMEM_BYTES - 8 * 1024 * 1024


def _rmsnorm(x, g):
    inv = lax.rsqrt(jnp.mean(x * x, axis=-1, keepdims=True) + EPS)
    return x * inv * g


def _dot(a, b):
    return jnp.dot(a, b, preferred_element_type=F32)


def _turn_pitch(tseq):
    assert tseq % V7X_F32_SUBLANES == 0
    return tseq + V7X_F32_SUBLANES // 2


def _resident(block_shape, index_map):
    return pl.BlockSpec(block_shape, index_map, pipeline_mode=pl.Buffered(1))


def _s5_regroup_in(u, n_chunks):
    crow = n_chunks * BATCH_TILE
    u_pos = u.reshape(n_chunks, CHUNK, BATCH_TILE, SSM_WIDTH)
    u_pos = [u_pos[:, p].reshape(crow, SSM_WIDTH) for p in range(CHUNK)]
    return [jnp.concatenate([up[:, gp * PAIR_U:(gp + 1) * PAIR_U] for up in u_pos], axis=1).astype(BF16)
            for gp in range(N_PAIRS)]


def _s5_chunk_inputs(xg, w1_ref):
    zy = [_dot(xg[gp], w1_ref[gp]) for gp in range(N_PAIRS)]
    z_re = jnp.concatenate([m[:, 0:PAIR_STATE] for m in zy], axis=1)
    z_im = jnp.concatenate([m[:, PAIR_STATE:2 * PAIR_STATE] for m in zy], axis=1)
    return z_re, z_im, [m[:, 2 * PAIR_STATE:] for m in zy]


def _s5_scan(z_re, z_im, hre_scr, him_scr, a8r_ref, a8i_ref, n_chunks):
    a_re = jnp.broadcast_to(a8r_ref[...], (BATCH_TILE, SSM_LANES))
    a_im = jnp.broadcast_to(a8i_ref[...], (BATCH_TILE, SSM_LANES))
    h_re, h_im = hre_scr[...], him_scr[...]
    prev_re, prev_im = [], []
    for c in range(n_chunks):
        prev_re.append(h_re)
        prev_im.append(h_im)
        seqs = slice(c * BATCH_TILE, (c + 1) * BATCH_TILE)
        h_re, h_im = (a_re * h_re - a_im * h_im + z_re[seqs], a_re * h_im + a_im * h_re + z_im[seqs])
    hre_scr[...] = h_re
    him_scr[...] = h_im
    return jnp.concatenate(prev_re, axis=0), jnp.concatenate(prev_im, axis=0)


def _s5_chunk_outputs(y_in, prev_re, prev_im, w2_ref):
    y_pair = []
    for gp in range(N_PAIRS):
        lanes = slice(gp * PAIR_STATE, (gp + 1) * PAIR_STATE)
        sp = jnp.concatenate([prev_re[:, lanes], prev_im[:, lanes]], axis=1).astype(BF16)
        y_pair.append((y_in[gp] + _dot(sp, w2_ref[gp])).astype(BF16))
    return y_pair


def _s5_regroup_out(y_pair, n_chunks):
    y_pos = [jnp.concatenate([yp[:, q * PAIR_U:(q + 1) * PAIR_U] for yp in y_pair], axis=1).astype(F32)
             .reshape(n_chunks, BATCH_TILE, SSM_WIDTH) for q in range(CHUNK)]
    return jnp.stack(y_pos, axis=1).reshape(n_chunks * CHUNK * BATCH_TILE, SSM_WIDTH)


def _mixer_body(x_ref, conv0_ref, hre0_ref, him0_ref, gmix_ref, win_ref, convw_ref, a8r_ref, a8i_ref, w1_ref, w2_ref,
                dskip_ref, wglu_ref, ga_ref, gb_ref, wout_ref,
                o_ref, convo_ref, hreo_ref, himo_ref,
                conv_scr, hre_scr, him_scr, turn_scr, *, tseq):
    rows = BATCH_TILE * tseq

    @pl.when(pl.program_id(1) == 0)
    def _():
        conv_scr[...] = conv0_ref[...].reshape((CONV_K - 1) * BATCH_TILE, CONV_CH)
        hre_scr[...] = hre0_ref[...]
        him_scr[...] = him0_ref[...]

    pitch = _turn_pitch(tseq)
    gmix = gmix_ref[...]
    n_parts = HEAD_PARTS if tseq == SEQ_TILE else 1
    part = tseq // n_parts
    xn, u = [], []
    for i in range(n_parts):
        for b in range(BATCH_TILE):
            xn_b = _rmsnorm(x_ref[b, i * part:(i + 1) * part], gmix)
            for j in range(LANE_SLABS):
                turn_scr[j, pl.ds(b * pitch + i * part, part), :] = xn_b[:, j * V7X_LANES:(j + 1) * V7X_LANES]
        xn.append(jnp.concatenate(
            [jnp.concatenate([turn_scr[j, pl.ds(t, BATCH_TILE, stride=pitch), :]
                              for t in range(i * part, (i + 1) * part)], axis=0)
             for j in range(LANE_SLABS)], axis=1).astype(BF16))
        u.append(_dot(xn[i], win_ref[:, 3 * CONV_CH:]))
    u = jnp.concatenate(u, axis=0)

    def in_proj(block):
        cols = slice(block * CONV_CH, (block + 1) * CONV_CH)
        return jnp.concatenate([_dot(xn_i, win_ref[:, cols]) for xn_i in xn], axis=0)

    n_chunks = tseq // CHUNK
    xg = _s5_regroup_in(u, n_chunks)
    c_gate = in_proj(1)
    z_re, z_im, y_in = _s5_chunk_inputs(xg, w1_ref)
    v = in_proj(2)
    prev_re, prev_im = _s5_scan(z_re, z_im, hre_scr, him_scr, a8r_ref, a8i_ref, n_chunks)

    xin = c_gate * v
    xp = jnp.concatenate([conv_scr[...], xin], axis=0)
    w = convw_ref[...]
    conv = w[0:1] * xp[0:rows]
    for k in range(1, CONV_K):
        conv = conv + w[k:k + 1] * xp[k * BATCH_TILE:k * BATCH_TILE + rows]
    conv_scr[...] = xp[rows:]

    y_pair = _s5_chunk_outputs(y_in, prev_re, prev_im, w2_ref)
    b_gate = in_proj(0)
    y_s = _s5_regroup_out(y_pair, n_chunks) + dskip_ref[...] * u
    y_a = _rmsnorm(b_gate * conv, ga_ref[...]).astype(BF16)
    y_s = jax.nn.gelu(y_s)
    gate = _dot(y_s.astype(BF16), wglu_ref[...])
    upd_a = _dot(y_a, wout_ref[0:CONV_CH, :])
    y_b = _rmsnorm(y_s * jax.nn.sigmoid(gate), gb_ref[...]).astype(BF16)

    slabs_per_block = V7X_MXU_DIM // V7X_LANES
    for n in range(D_MODEL // V7X_MXU_DIM):
        cols = slice(n * V7X_MXU_DIM, (n + 1) * V7X_MXU_DIM)
        upd = upd_a[:, cols] + _dot(y_b, wout_ref[CONV_CH:, cols])
        for k in range(slabs_per_block):
            j = n * slabs_per_block + k
            for t in range(tseq):
                turn_scr[j, pl.ds(t, BATCH_TILE, stride=pitch), :] = upd[t * BATCH_TILE:(t + 1) * BATCH_TILE,
                                                                         k * V7X_LANES:(k + 1) * V7X_LANES]
        for b in range(BATCH_TILE):
            o_ref[b, :, cols] = x_ref[b, :, cols] + jnp.concatenate(
                [turn_scr[n * slabs_per_block + k, pl.ds(b * pitch, tseq), :] for k in range(slabs_per_block)], axis=1)
    convo_ref[...] = conv_scr[...].reshape(CONV_K - 1, BATCH_TILE, CONV_CH)
    hreo_ref[...] = hre_scr[...]
    himo_ref[...] = him_scr[...]


def _mixer(x, conv0, h0_re, h0_im, lw, layer):
    batch, seq, _ = x.shape
    tseq = min(SEQ_TILE, seq)
    assert batch % BATCH_TILE == 0 and seq % tseq == 0 and tseq % CHUNK == 0

    def layer_block(*shape):
        return _resident((None,) + shape, lambda g, s: (layer,) + (0,) * len(shape))

    state_spec = pl.BlockSpec((BATCH_TILE, SSM_LANES), lambda g, s: (g, 0))
    conv_spec = pl.BlockSpec((CONV_K - 1, BATCH_TILE, CONV_CH), lambda g, s: (0, g, 0))
    x_spec = pl.BlockSpec((BATCH_TILE, tseq, D_MODEL), lambda g, s: (g, s, 0))
    return pl.pallas_call(
        functools.partial(_mixer_body, tseq=tseq),
        grid=(batch // BATCH_TILE, seq // tseq),
        in_specs=[
            x_spec, conv_spec, state_spec, state_spec,
            layer_block(1, D_MODEL),
            layer_block(D_MODEL, IN_PROJ_WIDTH),
            layer_block(CONV_K, CONV_CH),
            layer_block(1, SSM_LANES),
            layer_block(1, SSM_LANES),
            layer_block(N_PAIRS, V7X_MXU_DIM, 2 * V7X_MXU_DIM),
            layer_block(N_PAIRS, V7X_MXU_DIM, V7X_MXU_DIM),
            layer_block(1, SSM_WIDTH),
            layer_block(SSM_WIDTH, SSM_WIDTH),
            layer_block(1, CONV_CH),
            layer_block(1, SSM_WIDTH),
            layer_block(D_MODEL, D_MODEL),
        ],
        out_specs=[x_spec, conv_spec, state_spec, state_spec],
        out_shape=[
            jax.ShapeDtypeStruct(x.shape, F32),
            jax.ShapeDtypeStruct(conv0.shape, F32),
            jax.ShapeDtypeStruct(h0_re.shape, F32),
            jax.ShapeDtypeStruct(h0_im.shape, F32),
        ],
        scratch_shapes=[
            pltpu.VMEM(((CONV_K - 1) * BATCH_TILE, CONV_CH), F32),
            pltpu.VMEM((BATCH_TILE, SSM_LANES), F32),
            pltpu.VMEM((BATCH_TILE, SSM_LANES), F32),
            pltpu.VMEM((LANE_SLABS, BATCH_TILE * _turn_pitch(tseq), V7X_LANES), F32),
        ],
        compiler_params=pltpu.CompilerParams(
            dimension_semantics=("arbitrary", "arbitrary"), vmem_limit_bytes=VMEM_LIMIT),
        name=f"mixer_l{layer}_s{seq}",
    )(x, conv0, h0_re, h0_im, lw["g_mix"], lw["w_in"], lw["conv_w"], lw["a8_re"], lw["a8_im"], lw["w1"], lw["w2"],
      lw["d_skip"], lw["w_glu"], lw["g_grp_a"], lw["g_grp_b"], lw["w_out"])


def _attend_body(x_ref, k_ref, v_ref, g_ref, wq_ref, wo_ref, o_ref, *, bt, tq):
    x = x_ref[...].reshape(bt * tq, D_MODEL)
    xn = _rmsnorm(x, g_ref[...]).astype(BF16)
    q = (_dot(xn, wq_ref[...]) * (MEM_HEAD_DIM ** -0.5)).astype(BF16)
    seqs = []
    for i in range(bt):
        rows = slice(i * tq, (i + 1) * tq)
        k = k_ref[i].astype(BF16)
        v = v_ref[i].astype(BF16)
        heads = []
        for h in range(MEM_HEADS):
            lanes = slice(h * MEM_HEAD_DIM, (h + 1) * MEM_HEAD_DIM)
            s = lax.dot_general(q[rows, lanes], k[:, lanes], (((1,), (1,)), ((), ())), preferred_element_type=F32)
            p = jnp.exp(s - jnp.max(s, axis=-1, keepdims=True))
            o = _dot(p.astype(BF16), v[:, lanes]) / jnp.sum(p, axis=-1, keepdims=True)
            heads.append(o.astype(BF16))
        seqs.append(jnp.concatenate(heads, axis=1))
    out = x + _dot(jnp.concatenate(seqs, axis=0), wo_ref[...])
    o_ref[...] = out.reshape(bt, tq, D_MODEL)


def _attend(x, mem_k, mem_v, lw, layer):
    batch, seq, _ = x.shape
    tq = min(ATTEND_SEQ_TILE, seq)
    bt = min(max(ROW_TILE // tq, 1), batch)
    assert seq % tq == 0 and batch % bt == 0
    kv_spec = pl.BlockSpec((None, bt, N_MEM, D_MODEL), lambda b, s: (layer, b, 0, 0))
    x_spec = pl.BlockSpec((bt, tq, D_MODEL), lambda b, s: (b, s, 0))
    return pl.pallas_call(
        functools.partial(_attend_body, bt=bt, tq=tq),
        grid=(batch // bt, seq // tq),
        in_specs=[
            x_spec,
            kv_spec,
            kv_spec,
            _resident((None, 1, D_MODEL), lambda b, s: (layer, 0, 0)),
            _resident((None, D_MODEL, D_MODEL), lambda b, s: (layer, 0, 0)),
            _resident((None, D_MODEL, D_MODEL), lambda b, s: (layer, 0, 0)),
        ],
        out_specs=x_spec,
        out_shape=jax.ShapeDtypeStruct(x.shape, F32),
        compiler_params=pltpu.CompilerParams(
            dimension_semantics=("arbitrary", "arbitrary"), vmem_limit_bytes=VMEM_LIMIT),
        name=f"attend_l{layer}_s{seq}",
    )(x, mem_k, mem_v, lw["g_xattn"], lw["w_q"], lw["w_o"])


def _mlp_body(x_ref, g_ref, wup_ref, wdown_ref, gfinal_ref, o_ref, *, final_norm):
    x = x_ref[...]
    xn = _rmsnorm(x, g_ref[...]).astype(BF16)
    out = x
    hid = D_FF // MLP_HIDDEN_PARTS
    for c in range(MLP_HIDDEN_PARTS):
        hdn = jnp.square(jnp.maximum(_dot(xn, wup_ref[:, c * hid:(c + 1) * hid]), 0.0)).astype(BF16)
        out = out + _dot(hdn, wdown_ref[c * hid:(c + 1) * hid, :])
    if final_norm:
        out = _rmsnorm(out, gfinal_ref[...])
    o_ref[...] = out


def _mlp(x, lw, g_final, layer, final_norm):
    shape = x.shape
    x2 = x.reshape(-1, D_MODEL)
    n_rows = x2.shape[0]
    tm = min(MLP_ROW_TILE, n_rows)
    assert n_rows % tm == 0
    out = pl.pallas_call(
        functools.partial(_mlp_body, final_norm=final_norm),
        grid=(n_rows // tm,),
        in_specs=[
            pl.BlockSpec((tm, D_MODEL), lambda i: (i, 0)),
            _resident((None, 1, D_MODEL), lambda i: (layer, 0, 0)),
            _resident((None, D_MODEL, D_FF), lambda i: (layer, 0, 0)),
            _resident((None, D_FF, D_MODEL), lambda i: (layer, 0, 0)),
            _resident((1, D_MODEL), lambda i: (0, 0)),
        ],
        out_specs=pl.BlockSpec((tm, D_MODEL), lambda i: (i, 0)),
        out_shape=jax.ShapeDtypeStruct(x2.shape, F32),
        compiler_params=pltpu.CompilerParams(
            dimension_semantics=("arbitrary",), vmem_limit_bytes=VMEM_LIMIT),
        name=f"mlp_l{layer}_r{n_rows}",
    )(x2, lw["g_mlp"], lw["w_up"], lw["w_down"], g_final)
    return out.reshape(shape)


def _memkv_body(mem_ref, g_ref, wk_ref, wv_ref, k_ref, v_ref, kb_ref, vb_ref):
    mn = _rmsnorm(mem_ref[...].reshape(MEMKV_BATCH_TILE * N_MEM, D_MODEL), g_ref[...]).astype(BF16)
    for w_ref, heads_ref, flat_ref in ((wk_ref, k_ref, kb_ref), (wv_ref, v_ref, vb_ref)):
        kv = _dot(mn, w_ref[...])
        flat_ref[...] = kv.reshape(MEMKV_BATCH_TILE, N_MEM, D_MODEL).astype(BF16)
        for h in range(MEM_HEADS):
            heads_ref[:, :, h, :] = kv[:, h * MEM_HEAD_DIM:(h + 1) * MEM_HEAD_DIM].reshape(
                MEMKV_BATCH_TILE, N_MEM, MEM_HEAD_DIM)


def _memkv(mem, g_mem, w_k, w_v):
    batch = mem.shape[0]
    depth = w_k.shape[0]
    assert batch % MEMKV_BATCH_TILE == 0
    heads_shape = jax.ShapeDtypeStruct((depth, batch, N_MEM, MEM_HEADS, MEM_HEAD_DIM), F32)
    heads_spec = pl.BlockSpec((None, MEMKV_BATCH_TILE, N_MEM, MEM_HEADS, MEM_HEAD_DIM), lambda l, b: (l, b, 0, 0, 0))
    flat_shape = jax.ShapeDtypeStruct((depth, batch, N_MEM, D_MODEL), BF16)
    flat_spec = pl.BlockSpec((None, MEMKV_BATCH_TILE, N_MEM, D_MODEL), lambda l, b: (l, b, 0, 0))
    w_spec = pl.BlockSpec((None, D_MODEL, D_MODEL), lambda l, b: (l, 0, 0))
    return pl.pallas_call(
        _memkv_body,
        grid=(depth, batch // MEMKV_BATCH_TILE),
        in_specs=[
            pl.BlockSpec((MEMKV_BATCH_TILE, N_MEM, D_MODEL), lambda l, b: (b, 0, 0)),
            pl.BlockSpec((None, 1, D_MODEL), lambda l, b: (l, 0, 0)),
            w_spec,
            w_spec,
        ],
        out_specs=[heads_spec, heads_spec, flat_spec, flat_spec],
        out_shape=[heads_shape, heads_shape, flat_shape, flat_shape],
        compiler_params=pltpu.CompilerParams(
            dimension_semantics=("arbitrary", "arbitrary"), vmem_limit_bytes=VMEM_LIMIT),
        name="memkv",
    )(mem, g_mem, w_k, w_v)


def _s5_fold_body(are_ref, aim_ref, logdt_ref, bre_ref, bim_ref, cre_ref, cim_ref,
                  w1_ref, w2_ref, a8re_ref, a8im_ref, w1_scr, w2t_scr):
    for i in range(FOLD_PAIRS):
        groups = pl.ds(i * PAIR_GROUPS, PAIR_GROUPS)
        lanes = pl.ds(i * PAIR_STATE, PAIR_STATE)
        _s5_fold_pair(*(r.at[groups] for r in (are_ref, aim_ref, logdt_ref, bre_ref, bim_ref, cre_ref, cim_ref)),
                      w1_ref.at[i], w2_ref.at[i], a8re_ref.at[:, lanes], a8im_ref.at[:, lanes], w1_scr, w2t_scr)


def _s5_fold_pair(are_ref, aim_ref, logdt_ref, bre_ref, bim_ref, cre_ref, cim_ref,
                  w1_ref, w2_ref, a8re_ref, a8im_ref, w1_scr, w2t_scr):
    w1_scr[...] = jnp.zeros_like(w1_scr)
    w2t_scr[...] = jnp.zeros_like(w2t_scr)
    k = lax.broadcasted_iota(jnp.int32, (2 * CHUNK, SSM_STATE), 0).astype(F32)
    for g in range(PAIR_GROUPS):
        lam_re = jnp.minimum(are_ref[g], A_RE_MAX)
        lam_im = aim_ref[g]
        dt = jnp.exp(logdt_ref[g])
        mag = jnp.exp(k * (lam_re * dt))
        ang = k * (lam_im * dt)
        p_re, p_im = mag * jnp.cos(ang), mag * jnp.sin(ang)
        a8re_ref[:, g * SSM_STATE:(g + 1) * SSM_STATE] = p_re[CHUNK:CHUNK + 1]
        a8im_ref[:, g * SSM_STATE:(g + 1) * SSM_STATE] = p_im[CHUNK:CHUNK + 1]
        den = lam_re * lam_re + lam_im * lam_im
        n_re, n_im = p_re[1:2] - 1.0, p_im[1:2]
        f_re = (n_re * lam_re + n_im * lam_im) / den
        f_im = (n_im * lam_re - n_re * lam_im) / den
        b_re, b_im = bre_ref[g], bim_ref[g]
        bb_re, bb_im = f_re * b_re - f_im * b_im, f_re * b_im + f_im * b_re
        c_re, c_im = cre_ref[g], cim_ref[g]
        e_re, e_im = [], []
        for j in range(CHUNK):
            pj_re, pj_im = p_re[j:j + 1], p_im[j:j + 1]
            e_re.append(pj_re * bb_re - pj_im * bb_im)
            e_im.append(pj_re * bb_im + pj_im * bb_re)
            p = CHUNK - 1 - j
            for part, e in enumerate((e_re[j], e_im[j])):
                w1_scr[pl.ds(p * PAIR_U + g * SSM_GROUP, SSM_GROUP),
                       pl.ds(part * PAIR_STATE + g * SSM_STATE, SSM_STATE)] = e
            r_re, r_im = p_re[j + 1:j + 2], p_im[j + 1:j + 2]
            for part, d in enumerate((c_re * r_re - c_im * r_im, -(c_re * r_im + c_im * r_re))):
                w2t_scr[pl.ds(j * PAIR_U + g * SSM_GROUP, SSM_GROUP),
                        pl.ds(part * PAIR_STATE + g * SSM_STATE, SSM_STATE)] = d
        nt = (((1,), (1,)), ((), ()))
        kern = (lax.dot_general(jnp.concatenate(e_re, axis=0), c_re, nt, precision=lax.Precision.HIGHEST,
                                preferred_element_type=F32)
                - lax.dot_general(jnp.concatenate(e_im, axis=0), c_im, nt, precision=lax.Precision.HIGHEST,
                                  preferred_element_type=F32))
        for j in range(CHUNK):
            k_j = kern[j * SSM_GROUP:(j + 1) * SSM_GROUP]
            for p in range(CHUNK - j):
                w1_scr[pl.ds(p * PAIR_U + g * SSM_GROUP, SSM_GROUP),
                       pl.ds(2 * PAIR_STATE + (p + j) * PAIR_U + g * SSM_GROUP, SSM_GROUP)] = k_j
    w1_ref[...] = w1_scr[...].astype(BF16)
    w2_ref[...] = w2t_scr[...].T.astype(BF16)


def _s5_fold(a_re, a_im, log_dt, b_re, b_im, c_re, c_im):
    depth = a_re.shape[0]
    row = lambda m: m.reshape(depth, SSM_GROUPS, 1, SSM_STATE)
    logdt = jnp.broadcast_to(log_dt[..., None, None], (depth, SSM_GROUPS, 1, SSM_STATE))
    vec_spec = pl.BlockSpec((None, FOLD_PAIRS * PAIR_GROUPS, 1, SSM_STATE), lambda l, n: (l, n, 0, 0))
    mat_spec = pl.BlockSpec((None, FOLD_PAIRS * PAIR_GROUPS, SSM_GROUP, SSM_STATE), lambda l, n: (l, n, 0, 0))
    a8_spec = pl.BlockSpec((None, 1, FOLD_PAIRS * PAIR_STATE), lambda l, n: (l, 0, n))
    a8_shape = jax.ShapeDtypeStruct((depth, 1, SSM_LANES), F32)
    return pl.pallas_call(
        _s5_fold_body,
        grid=(depth, N_PAIRS // FOLD_PAIRS),
        in_specs=[vec_spec, vec_spec, vec_spec, mat_spec, mat_spec, mat_spec, mat_spec],
        out_specs=[
            pl.BlockSpec((None, FOLD_PAIRS, V7X_MXU_DIM, 2 * V7X_MXU_DIM), lambda l, n: (l, n, 0, 0)),
            pl.BlockSpec((None, FOLD_PAIRS, V7X_MXU_DIM, V7X_MXU_DIM), lambda l, n: (l, n, 0, 0)),
            a8_spec, a8_spec,
        ],
        out_shape=[
            jax.ShapeDtypeStruct((depth, N_PAIRS, V7X_MXU_DIM, 2 * V7X_MXU_DIM), BF16),
            jax.ShapeDtypeStruct((depth, N_PAIRS, V7X_MXU_DIM, V7X_MXU_DIM), BF16),
            a8_shape, a8_shape,
        ],
        scratch_shapes=[
            pltpu.VMEM((V7X_MXU_DIM, 2 * V7X_MXU_DIM), F32),
            pltpu.VMEM((V7X_MXU_DIM, V7X_MXU_DIM), F32),
        ],
        compiler_params=pltpu.CompilerParams(dimension_semantics=("arbitrary", "arbitrary")),
        name="s5_fold",
    )(row(a_re), row(a_im), logdt, jnp.swapaxes(b_re, -1, -2), jnp.swapaxes(b_im, -1, -2), c_re, c_im)


def kernel(x_prompt, x_sample, mem_prompt, state_conv, state_ssm_re, state_ssm_im, cache_mem_k, cache_mem_v, g_mix, w_in, conv_w, ssm_a_re, ssm_a_im, ssm_log_dt, ssm_b_re, ssm_b_im, ssm_c_re, ssm_c_im, ssm_d, w_glu, g_grp_a, g_grp_b, w_out, g_xattn, g_mem, w_q, w_k, w_v, w_o, g_mlp, w_up, w_down, g_final):
    depth = w_in.shape[0]
    bp = x_prompt.shape[0]
    bs = x_sample.shape[0]
    w1, w2, a8_re, a8_im = _s5_fold(ssm_a_re, ssm_a_im, ssm_log_dt, ssm_b_re, ssm_b_im, ssm_c_re, ssm_c_im)
    row = lambda g: g.reshape(depth, 1, -1)
    lw = {
        "g_mix": row(g_mix), "w_in": w_in.astype(BF16), "conv_w": conv_w, "a8_re": a8_re, "a8_im": a8_im,
        "w1": w1, "w2": w2, "d_skip": row(ssm_d), "w_glu": w_glu.astype(BF16),
        "g_grp_a": row(g_grp_a), "g_grp_b": row(g_grp_b), "w_out": w_out.astype(BF16),
        "g_xattn": row(g_xattn), "w_q": w_q.astype(BF16), "w_o": w_o.astype(BF16),
        "g_mlp": row(g_mlp), "w_up": w_up.astype(BF16), "w_down": w_down.astype(BF16),
    }
    g_final2 = g_final.reshape(1, D_MODEL)

    mk_out, mv_out, mk_p, mv_p = _memkv(mem_prompt, row(g_mem), w_k.astype(BF16), w_v.astype(BF16))
    mk_s = cache_mem_k.reshape(depth, bs, N_MEM, D_MODEL)
    mv_s = cache_mem_v.reshape(depth, bs, N_MEM, D_MODEL)

    zero_conv = jnp.zeros((CONV_K - 1, bp, CONV_CH), F32)
    zero_h = jnp.zeros((bp, SSM_LANES), F32)
    state_shape = (-1, SSM_GROUPS, SSM_STATE)

    xp, xs = x_prompt, x_sample
    conv_p, re_p, im_p, conv_s, re_s, im_s = [], [], [], [], [], []
    for l in range(depth):
        last = l == depth - 1
        xp, c_new, r_new, i_new = _mixer(xp, zero_conv, zero_h, zero_h, lw, l)
        conv_p.append(jnp.swapaxes(c_new, 0, 1))
        re_p.append(r_new.reshape(state_shape))
        im_p.append(i_new.reshape(state_shape))
        xp = _attend(xp, mk_p, mv_p, lw, l)
        xp = _mlp(xp, lw, g_final2, l, last)

        xs, c_new, r_new, i_new = _mixer(xs, jnp.swapaxes(state_conv[l], 0, 1),
                                         state_ssm_re[l].reshape(bs, SSM_LANES),
                                         state_ssm_im[l].reshape(bs, SSM_LANES), lw, l)
        conv_s.append(jnp.swapaxes(c_new, 0, 1))
        re_s.append(r_new.reshape(state_shape))
        im_s.append(i_new.reshape(state_shape))
        xs = _attend(xs, mk_s, mv_s, lw, l)
        xs = _mlp(xs, lw, g_final2, l, last)

    return (xp, xs,
            jnp.stack(conv_p), jnp.stack(re_p), jnp.stack(im_p), mk_out, mv_out,
            jnp.stack(conv_s), jnp.stack(re_s), jnp.stack(im_s))
```
